```python
import jax, jax.numpy as jnp
from jax import lax
import numpy as np

D_MODEL = 1024
BATCH = 8
SEQ = 8192
DEPTH = 2
DEC_BATCH = 8
DEC_SEQ = 16
PAST_LEN = 4096

CHUNK = 64
D_MIX = D_MODEL
CONV_DIM = D_MIX // 2
CONV_W = 3
N_HEADS = 8
HEAD_DIM = 64
ATTN_DIM = N_HEADS * HEAD_DIM
N_KV = 2
N_IDX_HEADS = 8
IDX_DIM = 32
TOPK_MAX = 256
Q_BLOCK = 128
FF_DENSE = 2816
N_EXPERTS = 8
TOP_K_EXPERTS = 2
FF_EXPERT = 1408
N_DENSE = (DEPTH + 1) // 2
N_MOE = DEPTH // 2
EPS = 1e-6
COL_SIZES = (CONV_DIM, CONV_DIM, CONV_DIM, ATTN_DIM, N_KV * HEAD_DIM, N_KV * HEAD_DIM,
             N_IDX_HEADS * IDX_DIM, IDX_DIM, N_IDX_HEADS)
IN_COLS = sum(COL_SIZES)

kernel_name = "hybrid_shortconv_dsa_stream_step"


def rmsnorm(x, g):
    xf = x.astype(jnp.float32)
    y = xf * lax.rsqrt(jnp.mean(xf * xf, axis=-1, keepdims=True) + EPS)
    return (y * g.astype(jnp.float32)).astype(x.dtype)


def project(x, norm_g, w_in, q_g, k_g, iq_g, ik_g):
    b, t = x.shape[:2]
    h = rmsnorm(x, norm_g) @ w_in
    offs = [int(o) for o in np.cumsum(COL_SIZES)[:-1]]
    c_gate, b_gate, u, q, k, v, iq, ik, iw = jnp.split(h, offs, axis=-1)
    q = rmsnorm(q.reshape(b, t, N_HEADS, HEAD_DIM), q_g)
    k = rmsnorm(k.reshape(b, t, N_KV, HEAD_DIM), k_g)
    v = v.reshape(b, t, N_KV, HEAD_DIM)
    iq = rmsnorm(iq.reshape(b, t, N_IDX_HEADS, IDX_DIM), iq_g)
    ik = rmsnorm(ik, ik_g)
    conv_in = c_gate * u
    return conv_in, b_gate, q, k, v, iq, ik, iw


def causal_conv(conv_in, prev, w):
    t = conv_in.shape[1]
    p = jnp.concatenate([prev, conv_in], axis=1)
    out = sum(w[j] * p[:, j:j + t] for j in range(CONV_W))
    return out, p[:, -(CONV_W - 1):]


def dsa_block(q, iq, iw, q_pos, k_all, v_all, ik_all, topk):
    b, tq = q.shape[:2]
    k_pos = jnp.arange(k_all.shape[1])
    admissible = (k_pos[None, :] // CHUNK) <= (q_pos[:, None] // CHUNK)
    logits = jnp.einsum('bqhd,bsd->bqhs', iq, ik_all).astype(jnp.float32) * (IDX_DIM ** -0.5)
    w = iw.astype(jnp.float32) * (N_IDX_HEADS ** -0.5)
    score = jnp.einsum('bqh,bqhs->bqs', w, jax.nn.relu(logits))
    score = jnp.where(admissible[None], score, -jnp.inf)
    _, idx = lax.top_k(score, topk)
    valid = (idx // CHUNK) <= (q_pos[None, :, None] // CHUNK)
    gather = jax.vmap(lambda a, i: a[i])
    k_sel = gather(k_all, idx)
    v_sel = gather(v_all, idx)
    qg = q.reshape(b, tq, N_KV, N_HEADS // N_KV, HEAD_DIM)
    s = jnp.einsum('bqgrd,bqkgd->bqgrk', qg, k_sel).astype(jnp.float32) * (HEAD_DIM ** -0.5)
    s = jnp.where(valid[:, :, None, None, :], s, -jnp.inf)
    p = jax.nn.softmax(s, axis=-1).astype(v_sel.dtype)
    o = jnp.einsum('bqgrk,bqkgd->bqgrd', p, v_sel)
    return o.reshape(b, tq, ATTN_DIM)


def dsa_prompt(q, iq, iw, k, v, ik):
    b, s = q.shape[:2]
    nb = s // Q_BLOCK
    topk = min(TOPK_MAX, s // 4)

    def blocks(a):
        return jnp.moveaxis(a.reshape(b, nb, Q_BLOCK, *a.shape[2:]), 1, 0)

    pos = jnp.arange(s).reshape(nb, Q_BLOCK)
    out = lax.map(lambda args: dsa_block(*args, k, v, ik, topk),
                  (blocks(q), blocks(iq), blocks(iw), pos))
    return jnp.moveaxis(out, 0, 1).reshape(b, s, ATTN_DIM)


def swiglu(x, wg, wu, wd):
    return (jax.nn.silu(x @ wg) * (x @ wu)) @ wd


def moe_ffn(x, w_router, e_gate, e_up, e_down):
    probs = jax.nn.softmax((x @ w_router).astype(jnp.float32), axis=-1)
    top_p, top_i = lax.top_k(probs, TOP_K_EXPERTS)
    top_p = top_p / jnp.sum(top_p, axis=-1, keepdims=True)
    gates = jnp.sum(jax.nn.one_hot(top_i, N_EXPERTS, dtype=jnp.float32) * top_p[..., None], axis=-2)
    gates = gates.astype(x.dtype)
    out = jnp.zeros_like(x)
    for e in range(N_EXPERTS):
        out = out + gates[..., e:e + 1] * swiglu(x, e_gate[e], e_up[e], e_down[e])
    return out


def setup_inputs(seed: int = 0) -> dict:
    key = jax.random.key(seed)
    ks = iter(jax.random.split(key, 32))

    def nrm(shape, scale=1.0):
        return jax.random.normal(next(ks), shape, jnp.float32) * scale

    def gain(shape):
        return 1.0 + 0.02 * jax.random.normal(next(ks), shape, jnp.float32)

    return {
        "x_prompt": nrm((BATCH, SEQ, D_MODEL)),
        "x_sample": nrm((DEC_BATCH, DEC_SEQ, D_MODEL)),
        "cache_k": nrm((DEPTH, DEC_BATCH, PAST_LEN, N_KV, HEAD_DIM)),
        "cache_v": nrm((DEPTH, DEC_BATCH, PAST_LEN, N_KV, HEAD_DIM)),
        "cache_ik": nrm((DEPTH, DEC_BATCH, PAST_LEN, IDX_DIM)),
        "state_conv": nrm((DEPTH, DEC_BATCH, CONV_W - 1, CONV_DIM)),
        "attn_norm_g": gain((DEPTH, D_MODEL)),
        "w_in": nrm((DEPTH, D_MODEL, IN_COLS), D_MODEL ** -0.5),
        "conv_w": nrm((DEPTH, CONV_W, CONV_DIM), CONV_W ** -0.5),
        "q_norm_g": gain((DEPTH, HEAD_DIM)),
        "k_norm_g": gain((DEPTH, HEAD_DIM)),
        "iq_norm_g": gain((DEPTH, IDX_DIM)),
        "ik_norm_g": gain((DEPTH, IDX_DIM)),
        "w_out": nrm((DEPTH, D_MIX, D_MODEL), D_MIX ** -0.5),
        "ffn_norm_g": gain((DEPTH, D_MODEL)),
        "ffn_w_gate": nrm((N_DENSE, D_MODEL, FF_DENSE), D_MODEL ** -0.5),
        "ffn_w_up": nrm((N_DENSE, D_MODEL, FF_DENSE), D_MODEL ** -0.5),
        "ffn_w_down": nrm((N_DENSE, FF_DENSE, D_MODEL), FF_DENSE ** -0.5),
        "router_w": nrm((N_MOE, D_MODEL, N_EXPERTS), D_MODEL ** -0.5),
        "moe_w_gate": nrm((N_MOE, N_EXPERTS, D_MODEL, FF_EXPERT), D_MODEL ** -0.5),
        "moe_w_up": nrm((N_MOE, N_EXPERTS, D_MODEL, FF_EXPERT), D_MODEL ** -0.5),
        "moe_w_down": nrm((N_MOE, N_EXPERTS, FF_EXPERT, D_MODEL), FF_EXPERT ** -0.5),
    }


def reference(x_prompt, x_sample, cache_k, cache_v, cache_ik, state_conv,
              attn_norm_g, w_in, conv_w, q_norm_g, k_norm_g, iq_norm_g, ik_norm_g, w_out,
              ffn_norm_g, ffn_w_gate, ffn_w_up, ffn_w_down,
              router_w, moe_w_gate, moe_w_up, moe_w_down):
    xp, xs = x_prompt, x_sample
    past = cache_k.shape[2]
    kp_l, vp_l, ikp_l, cp_l = [], [], [], []
    ks_l, vs_l, iks_l, cs_l = [], [], [], []
    for l in range(DEPTH):
        mix = (attn_norm_g[l], w_in[l], q_norm_g[l], k_norm_g[l], iq_norm_g[l], ik_norm_g[l])

        conv_in, b_gate, q, k, v, iq, ik, iw = project(xp, *mix)
        prev0 = jnp.zeros((xp.shape[0], CONV_W - 1, CONV_DIM), conv_in.dtype)
        conv_out, conv_st = causal_conv(conv_in, prev0, conv_w[l])
        attn = dsa_prompt(q, iq, iw, k, v, ik)
        xp = xp + jnp.concatenate([b_gate * conv_out, attn], axis=-1) @ w_out[l]
        kp_l.append(k); vp_l.append(v); ikp_l.append(ik); cp_l.append(conv_st)

        conv_in, b_gate, q, k, v, iq, ik, iw = project(xs, *mix)
        conv_out, conv_st = causal_conv(conv_in, state_conv[l].astype(conv_in.dtype), conv_w[l])
        k_all = jnp.concatenate([cache_k[l].astype(k.dtype), k], axis=1)
        v_all = jnp.concatenate([cache_v[l].astype(v.dtype), v], axis=1)
        ik_all = jnp.concatenate([cache_ik[l].astype(ik.dtype), ik], axis=1)
        topk = min(TOPK_MAX, k_all.shape[1] // 4)
        q_pos = past + jnp.arange(xs.shape[1])
        attn = dsa_block(q, iq, iw, q_pos, k_all, v_all, ik_all, topk)
        xs = xs + jnp.concatenate([b_gate * conv_out, attn], axis=-1) @ w_out[l]
        ks_l.append(k); vs_l.append(v); iks_l.append(ik); cs_l.append(conv_st)

        hp = rmsnorm(xp, ffn_norm_g[l])
        hs = rmsnorm(xs, ffn_norm_g[l])
        j = l // 2
        if l % 2 == 0:
            xp = xp + swiglu(hp, ffn_w_gate[j], ffn_w_up[j], ffn_w_down[j])
            xs = xs + swiglu(hs, ffn_w_gate[j], ffn_w_up[j], ffn_w_down[j])
        else:
            xp = xp + moe_ffn(hp, router_w[j], moe_w_gate[j], moe_w_up[j], moe_w_down[j])
            xs = xs + moe_ffn(hs, router_w[j], moe_w_gate[j], moe_w_up[j], moe_w_down[j])

    y_prompt, y_sample = xp, xs
    k_prompt, v_prompt = jnp.stack(kp_l), jnp.stack(vp_l)
    ik_prompt, conv_prompt = jnp.stack(ikp_l), jnp.stack(cp_l)
    k_sample, v_sample = jnp.stack(ks_l), jnp.stack(vs_l)
    ik_sample, conv_sample = jnp.stack(iks_l), jnp.stack(cs_l)
    return (y_prompt, y_sample, k_prompt, v_prompt, ik_prompt, conv_prompt,
            k_sample, v_sample, ik_sample, conv_sample)
```

```python
import functools
import math

import numpy as np
import jax
import jax.numpy as jnp
from jax import lax
from jax.experimental import pallas as pl
from jax.experimental.pallas import tpu as pltpu

CHUNK = 64
CONV_DIM = 512
CONV_W = 3
N_HEADS = 8
HEAD_DIM = 64
ATTN_DIM = N_HEADS * HEAD_DIM
N_KV = 2
N_IDX_HEADS = 8
IDX_DIM = 32
TOPK_MAX = 256
Q_BLOCK = 128
N_EXPERTS = 8
EPS = 1e-6
COL_SIZES = (CONV_DIM, CONV_DIM, CONV_DIM, ATTN_DIM, N_KV * HEAD_DIM, N_KV * HEAD_DIM,
             N_IDX_HEADS * IDX_DIM, IDX_DIM, N_IDX_HEADS)
IN_COLS = sum(COL_SIZES)

LANES = 128
V7X_VMEM_LIMIT = 56 * 1024 * 1024

IN_COLS_PAD = ((IN_COLS + LANES - 1) // LANES) * LANES
TAIL_OFF = IN_COLS_PAD - LANES
IW_LANE = IDX_DIM
QSCALE = HEAD_DIM ** -0.5 * math.log2(math.e)
IWSCALE = IDX_DIM ** -0.5 * N_IDX_HEADS ** -0.5
NEG = -1e30
INT_MIN = -2 ** 31
INT_MAX = 2 ** 31 - 1
KEY_CHUNK = 512

F32 = jnp.float32
BF16 = jnp.bfloat16
I32 = jnp.int32


def _split_hi_lo(x):
    hi = x.astype(BF16)
    lo = (x - hi.astype(F32)).astype(BF16)
    return hi, lo


def _group_rsqrt(x, g_ref):
    hi, lo = _split_hi_lo(x * x)
    g = g_ref[...]
    ms = jnp.dot(hi, g, preferred_element_type=F32) + jnp.dot(lo, g, preferred_element_type=F32)
    return lax.rsqrt(ms + EPS)


def _rmsnorm(x, g):
    return x * lax.rsqrt(jnp.mean(x * x, axis=-1, keepdims=True) + EPS) * g


def _inproj_kernel(x_ref, prev_ref, g_ref, w_ref, cw_ref, gq_ref, gk_ref, giq_ref, gik_ref,
                   mq_ref, mk_ref, miq_ref, mik_ref, s1_ref, s2_ref, t1_ref, t2_ref,
                   yconv_ref, q_ref, k_ref, v_ref, k2_ref, vb_ref, iqcat_ref, iw_ref, ik_ref,
                   ikcat_ref, cst_ref, carry_ref):
    t = pl.program_id(1)
    nt = pl.num_programs(1)
    x = x_ref[0]
    tm = x.shape[0]
    h = _rmsnorm(x, g_ref[...]).astype(BF16)
    p = jnp.dot(h, w_ref[...], preferred_element_type=F32)
    o = np.cumsum((0,) + COL_SIZES)
    c_gate, b_gate, u = p[:, o[0]:o[1]], p[:, o[1]:o[2]], p[:, o[2]:o[3]]
    q, k, v, iq = p[:, o[3]:o[4]], p[:, o[4]:o[5]], p[:, o[5]:o[6]], p[:, o[6]:o[7]]
    tail = p[:, TAIL_OFF:]

    ci = c_gate * u

    @pl.when(t == 0)
    def _():
        carry_ref[8 - (CONV_W - 1):, :] = prev_ref[0]

    row = lax.broadcasted_iota(I32, ci.shape, 0)
    c7 = carry_ref[7:8, :]
    c6 = carry_ref[6:7, :]
    s1 = jnp.where(row == 0, c7, pltpu.roll(ci, 1, axis=0))
    s2 = jnp.where(row == 0, c6, jnp.where(row == 1, c7, pltpu.roll(ci, 2, axis=0)))
    cw = cw_ref[...]
    conv = cw[0:1] * s2 + cw[1:2] * s1 + cw[2:3] * ci
    yconv_ref[0] = (b_gate * conv).astype(BF16)
    carry_ref[...] = ci[tm - 8:, :]

    @pl.when(t == nt - 1)
    def _():
        cst_ref[0] = ci[tm - (CONV_W - 1):, :]

    qn = q * _group_rsqrt(q, mq_ref) * gq_ref[...]
    q_ref[0] = (qn * QSCALE).astype(BF16)
    kn = k * _group_rsqrt(k, mk_ref) * gk_ref[...]
    k_ref[0] = kn
    v_ref[0] = v
    lane = lax.broadcasted_iota(I32, kn.shape, 1)
    kr = pltpu.roll(kn, HEAD_DIM, axis=1)
    k2_ref[0, :, 0:LANES] = jnp.where(lane < HEAD_DIM, kn, kr).astype(BF16)
    k2_ref[0, :, LANES:2 * LANES] = jnp.where(lane < HEAD_DIM, kr, kn).astype(BF16)
    vb_ref[0] = v.astype(BF16)

    iqn = iq * _group_rsqrt(iq, miq_ref) * giq_ref[...]
    ihi, ilo = _split_hi_lo(iqn)
    iqcat_ref[0] = (jnp.dot(ihi, s1_ref[...], preferred_element_type=F32)
                    + jnp.dot(ilo, s2_ref[...], preferred_element_type=F32)).astype(BF16)
    tn = tail * _group_rsqrt(tail, mik_ref) * gik_ref[...]
    ik_ref[0] = tn[:, 0:IDX_DIM]
    thi, tlo = _split_hi_lo(tn)
    ikcat_ref[0] = (jnp.dot(thi, t1_ref[...], preferred_element_type=F32)
                    + jnp.dot(tlo, t2_ref[...], preferred_element_type=F32)).astype(BF16)
    iw_ref[0] = tail * IWSCALE


def _const_spec(shape):
    nd = len(shape)
    return pl.BlockSpec(shape, lambda *_: (0,) * nd)


def _inproj(x, prev, lw, tm):
    b, t, d = x.shape
    grid = (b, t // tm)
    tile = lambda n: pl.BlockSpec((1, tm, n), lambda i, j: (i, j, 0))
    consts = [lw["attn_g"], lw["w_in"], lw["conv_w"], lw["gq"], lw["gk"], lw["giq"], lw["gik"],
              lw["mq"], lw["mk"], lw["miq"], lw["mik"], lw["s1"], lw["s2"], lw["t1"], lw["t2"]]
    in_specs = ([tile(d), pl.BlockSpec((1, CONV_W - 1, CONV_DIM), lambda i, j: (i, 0, 0))]
                + [_const_spec(c.shape) for c in consts])
    kvd = N_KV * HEAD_DIM
    out_shape = (
        jax.ShapeDtypeStruct((b, t, CONV_DIM), BF16),
        jax.ShapeDtypeStruct((b, t, ATTN_DIM), BF16),
        jax.ShapeDtypeStruct((b, t, kvd), F32),
        jax.ShapeDtypeStruct((b, t, kvd), F32),
        jax.ShapeDtypeStruct((b, t, 2 * kvd), BF16),
        jax.ShapeDtypeStruct((b, t, kvd), BF16),
        jax.ShapeDtypeStruct((b, t, N_IDX_HEADS * LANES), BF16),
        jax.ShapeDtypeStruct((b, t, LANES), F32),
        jax.ShapeDtypeStruct((b, t, IDX_DIM), F32),
        jax.ShapeDtypeStruct((b, t, LANES), BF16),
        jax.ShapeDtypeStruct((b, CONV_W - 1, CONV_DIM), F32),
    )
    out_specs = (tile(CONV_DIM), tile(ATTN_DIM), tile(kvd), tile(kvd), tile(2 * kvd), tile(kvd),
                 tile(N_IDX_HEADS * LANES), tile(LANES), tile(IDX_DIM), tile(LANES),
                 pl.BlockSpec((1, CONV_W - 1, CONV_DIM), lambda i, j: (i, 0, 0)))
    return pl.pallas_call(
        _inproj_kernel, grid=grid, in_specs=in_specs, out_specs=out_specs, out_shape=out_shape,
        scratch_shapes=[pltpu.VMEM((8, CONV_DIM), F32)],
        compiler_params=pltpu.CompilerParams(dimension_semantics=("arbitrary", "arbitrary"),
                                             vmem_limit_bytes=V7X_VMEM_LIMIT),
        name="inproj",
    )(x, prev, *consts)


def _slabs(n):
    return [slice(j * LANES, (j + 1) * LANES) for j in range(n // LANES)]


def _attn_kernel(q_ref, iqcat_ref, iw_ref, k2_ref, v_ref, ikcat_ref, o_ref,
                 keys_ref, lg_ref, lhs_ref, wb_ref, st_ref, cn_ref, qg_ref, m_ref, l_ref, acc_ref,
                 *, tq, kc, n_keys, q_pos0, topk):
    qi = pl.program_id(1)
    q0 = q_pos0 + qi * tq
    shift = CHUNK.bit_length() - 1
    lane = lax.broadcasted_iota(I32, (tq, LANES), 1)
    rowpos = q0 + lax.broadcasted_iota(I32, (tq, LANES), 0)
    lim = jnp.minimum(((rowpos >> shift) + 1) << shift, n_keys)
    lim_last = jnp.minimum((((q0 + tq - 1) >> shift) + 1) << shift, n_keys)
    n_chunks = (lim_last + kc - 1) // kc
    kfl = jnp.minimum(lim, topk).astype(F32)

    iqc = iqcat_ref[0]
    for h in range(N_IDX_HEADS):
        lhs_ref[h * tq:(h + 1) * tq, :] = iqc[:, h * LANES:(h + 1) * LANES]
    iw = iw_ref[0]
    for h in range(N_IDX_HEADS):
        col = jnp.sum(jnp.where(lane == IW_LANE + h, iw, 0.0), axis=1, keepdims=True)
        wb_ref[h] = jnp.broadcast_to(col, (tq, LANES))

    def score_chunk(c, carry):
        off = pl.multiple_of(c * kc, kc)
        ikc = ikcat_ref[0, pl.ds(off, kc), :]
        lg_ref[...] = lax.dot_general(lhs_ref[...], ikc, (((1,), (1,)), ((), ())),
                                      preferred_element_type=F32)
        for j, sl in enumerate(_slabs(kc)):
            sc = None
            for h in range(N_IDX_HEADS):
                term = jnp.maximum(lg_ref[h * tq:(h + 1) * tq, sl], 0.0) * wb_ref[h]
                sc = term if sc is None else sc + term
            sc = sc + 0.0
            bits = lax.bitcast_convert_type(sc, I32)
            key = bits ^ ((bits >> 31) & INT_MAX)
            kpos = off + j * LANES + lane
            keys_ref[c, :, sl] = jnp.where(kpos < lim, key, INT_MIN)
        return carry

    lax.fori_loop(0, n_chunks, score_chunk, 0)

    def count_ge(thr):
        def body(c, acc):
            kch = keys_ref[c]
            for sl in _slabs(kc):
                acc = acc + jnp.where(kch[:, sl] >= thr, 1.0, 0.0)
            return acc
        part = lax.fori_loop(0, n_chunks, body, jnp.zeros((tq, LANES), F32))
        return jnp.broadcast_to(jnp.sum(part, axis=1, keepdims=True), (tq, LANES))

    st_ref[0] = jnp.full((tq, LANES), INT_MIN + 1, I32)
    st_ref[1] = jnp.full((tq, LANES), INT_MAX, I32)
    cn_ref[0] = lim.astype(F32)
    cn_ref[1] = jnp.zeros((tq, LANES), F32)

    def bisect_cond(carry):
        it, pending = carry
        return jnp.logical_and(pending > 0.0, it < 40)

    def bisect_body(carry):
        it, _ = carry
        lo = st_ref[0]
        hi = st_ref[1]
        mid = (lo & hi) + ((lo ^ hi) >> 1)
        cnt = count_ge(mid)
        ge = cnt >= kfl
        exact = cnt == kfl
        lo_n = jnp.where(ge, mid, lo)
        hi_n = jnp.where(exact, mid + 1, jnp.where(ge, hi, mid))
        st_ref[0] = lo_n
        st_ref[1] = hi_n
        cn_ref[0] = jnp.where(ge, cnt, cn_ref[0])
        cn_ref[1] = jnp.where(ge, cn_ref[1], cnt)
        open_rows = jnp.where(hi_n != lo_n + 1, 1.0, 0.0)
        pending = jnp.max(jnp.max(open_rows, axis=1, keepdims=True), axis=0, keepdims=True)
        return it + 1, pending[0, 0]

    lax.while_loop(bisect_cond, bisect_body, (jnp.int32(0), jnp.float32(1.0)))

    thr = st_ref[0]
    extra = cn_ref[0] - kfl
    any_tie = jnp.max(jnp.max(extra, axis=1, keepdims=True), axis=0, keepdims=True)[0, 0]

    @pl.when(any_tie > 0.0)
    def _():
        need = kfl - cn_ref[1]

        def count_tied_upto(pos):
            def body(c, acc):
                kch = keys_ref[c]
                for j, sl in enumerate(_slabs(kc)):
                    kpos = c * kc + j * LANES + lane
                    hit = jnp.logical_and(kch[:, sl] == thr, kpos <= pos)
                    acc = acc + jnp.where(hit, 1.0, 0.0)
                return acc
            part = lax.fori_loop(0, n_chunks, body, jnp.zeros((tq, LANES), F32))
            return jnp.broadcast_to(jnp.sum(part, axis=1, keepdims=True), (tq, LANES))

        def pos_body(_, carry):
            plo, phi = carry
            mid = (plo + phi) >> 1
            ok = count_tied_upto(mid) >= need
            return jnp.where(ok, plo, mid), jnp.where(ok, mid, phi)

        n_steps = int(n_keys).bit_length() + 1
        _, last = lax.fori_loop(0, n_steps, pos_body,
                                (jnp.full((tq, LANES), -1, I32),
                                 jnp.full((tq, LANES), keys_ref.shape[0] * kc - 1, I32)))

        last = jnp.where(extra > 0.0, last, INT_MAX)

        def demote(c, carry):
            kch = keys_ref[c]
            for j, sl in enumerate(_slabs(kc)):
                kpos = c * kc + j * LANES + lane
                drop = jnp.logical_and(kch[:, sl] == thr, kpos > last)
                keys_ref[c, :, sl] = jnp.where(drop, kch[:, sl] - 1, kch[:, sl])
            return carry

        lax.fori_loop(0, n_chunks, demote, 0)

    qv = q_ref[0].astype(F32)
    heads_per_group = N_HEADS // N_KV
    rows = heads_per_group * tq
    for g in range(N_KV):
        for r in range(heads_per_group):
            hd = g * heads_per_group + r
            slab = qv[:, (hd // 2) * LANES:(hd // 2 + 1) * LANES]
            keep = (lane < HEAD_DIM) if hd % 2 == 0 else (lane >= HEAD_DIM)
            qg_ref[r * tq:(r + 1) * tq, :] = jnp.where(keep, slab, 0.0).astype(BF16)
        m_ref[...] = jnp.full((rows, LANES), NEG, F32)
        l_ref[...] = jnp.zeros((rows, LANES), F32)
        acc_ref[...] = jnp.zeros((rows, LANES), F32)

        def attend(c, carry, g=g):
            off = pl.multiple_of(c * kc, kc)
            kch = k2_ref[0, pl.ds(off, kc), g * LANES:(g + 1) * LANES]
            s = lax.dot_general(qg_ref[...], kch, (((1,), (1,)), ((), ())),
                                preferred_element_type=F32)
            keys = keys_ref[c]
            m_prev = m_ref[...]
            sm = []
            part = None
            for sl in _slabs(kc):
                bias = jnp.where(keys[:, sl] >= thr, 0.0, NEG)
                piece = s[:, sl] + jnp.concatenate([bias] * heads_per_group, axis=0)
                sm.append(piece)
                part = piece if part is None else jnp.maximum(part, piece)
            m_new = jnp.maximum(m_prev, jnp.max(part, axis=1, keepdims=True))
            alpha = jnp.exp2(m_prev - m_new)
            ps = [jnp.exp2(piece - m_new) for piece in sm]
            lsum = ps[0]
            for piece in ps[1:]:
                lsum = lsum + piece
            p = jnp.concatenate(ps, axis=1).astype(BF16)
            l_ref[...] = alpha * l_ref[...] + lsum
            acc_ref[...] = alpha * acc_ref[...] + jnp.dot(p, v_ref[0, pl.ds(off, kc), :],
                                                          preferred_element_type=F32)
            m_ref[...] = m_new
            return carry

        lax.fori_loop(0, n_chunks, attend, 0)

        out = acc_ref[...] / jnp.sum(l_ref[...], axis=1, keepdims=True)
        for jj in range(heads_per_group // 2):
            a = out[(2 * jj) * tq:(2 * jj + 1) * tq, :]
            bb = out[(2 * jj + 1) * tq:(2 * jj + 2) * tq, :]
            if g % 2 == 0:
                slab = jnp.where(lane < HEAD_DIM, a, pltpu.roll(bb, HEAD_DIM, axis=1))
            else:
                slab = jnp.where(lane < HEAD_DIM, pltpu.roll(a, HEAD_DIM, axis=1), bb)
            so = (g * heads_per_group // 2 + jj) * LANES
            o_ref[0, :, so:so + LANES] = slab.astype(BF16)


def _attention(q, iqcat, iw, k2, vb, ikcat, *, tq, n_keys, q_pos0):
    b, t, _ = q.shape
    lpad = k2.shape[1]
    kc = KEY_CHUNK
    assert lpad % kc == 0 and t % tq == 0 and CHUNK & (CHUNK - 1) == 0
    topk = min(TOPK_MAX, n_keys // 4)
    kern = functools.partial(_attn_kernel, tq=tq, kc=kc, n_keys=n_keys, q_pos0=q_pos0, topk=topk)
    qtile = lambda n: pl.BlockSpec((1, tq, n), lambda i, j: (i, j, 0))
    ktile = lambda n: pl.BlockSpec((1, lpad, n), lambda i, j: (i, 0, 0))
    rows = (N_HEADS // N_KV) * tq
    return pl.pallas_call(
        kern, grid=(b, t // tq),
        in_specs=[qtile(ATTN_DIM), qtile(N_IDX_HEADS * LANES), qtile(LANES),
                  ktile(2 * N_KV * HEAD_DIM), ktile(N_KV * HEAD_DIM), ktile(LANES)],
        out_specs=qtile(ATTN_DIM),
        out_shape=jax.ShapeDtypeStruct((b, t, ATTN_DIM), BF16),
        scratch_shapes=[
            pltpu.VMEM((lpad // kc, tq, kc), I32),
            pltpu.VMEM((N_IDX_HEADS * tq, kc), F32),
            pltpu.VMEM((N_IDX_HEADS * tq, LANES), BF16),
            pltpu.VMEM((N_IDX_HEADS, tq, LANES), F32),
            pltpu.VMEM((2, tq, LANES), I32),
            pltpu.VMEM((2, tq, LANES), F32),
            pltpu.VMEM((rows, LANES), BF16),
            pltpu.VMEM((rows, LANES), F32),
            pltpu.VMEM((rows, LANES), F32),
            pltpu.VMEM((rows, LANES), F32),
        ],
        compiler_params=pltpu.CompilerParams(dimension_semantics=("arbitrary", "arbitrary"),
                                             vmem_limit_bytes=V7X_VMEM_LIMIT),
        name="attention",
    )(q, iqcat, iw, k2, vb, ikcat)


def _mix_in(x_ref, yc_ref, at_ref, woa_ref, wob_ref, g_ref):
    x1 = (x_ref[...] + jnp.dot(yc_ref[...], woa_ref[...], preferred_element_type=F32)
          + jnp.dot(at_ref[...], wob_ref[...], preferred_element_type=F32))
    return x1, _rmsnorm(x1, g_ref[...])


def _swiglu(hb, wg, wu, wd):
    a = jnp.dot(hb, wg, preferred_element_type=F32)
    b = jnp.dot(hb, wu, preferred_element_type=F32)
    act = (a * (1.0 / (1.0 + jnp.exp(-a))) * b).astype(BF16)
    return jnp.dot(act, wd, preferred_element_type=F32)


def _ffn_dense_kernel(x_ref, yc_ref, at_ref, woa_ref, wob_ref, g_ref, wg_ref, wu_ref, wd_ref, o_ref):
    x1, hn = _mix_in(x_ref, yc_ref, at_ref, woa_ref, wob_ref, g_ref)
    o_ref[...] = x1 + _swiglu(hn.astype(BF16), wg_ref[...], wu_ref[...], wd_ref[...])


def _ffn_dense(x, yc, at, lw, tm):
    m, d = x.shape
    row = lambda n: pl.BlockSpec((tm, n), lambda i: (i, 0))
    consts = [lw["wo_a"], lw["wo_b"], lw["ffn_g"], lw["wg"], lw["wu"], lw["wd"]]
    return pl.pallas_call(
        _ffn_dense_kernel, grid=(m // tm,),
        in_specs=[row(d), row(CONV_DIM), row(ATTN_DIM)]
        + [pl.BlockSpec(c.shape, lambda i: (0, 0), pipeline_mode=pl.Buffered(1)) for c in consts],
        out_specs=row(d), out_shape=jax.ShapeDtypeStruct((m, d), F32),
        compiler_params=pltpu.CompilerParams(dimension_semantics=("arbitrary",),
                                             vmem_limit_bytes=V7X_VMEM_LIMIT),
        name="ffn_dense",
    )(x, yc, at, *consts)


def _ffn_moe_kernel(x_ref, yc_ref, at_ref, woa_ref, wob_ref, g_ref, rhi_ref, rlo_ref,
                    eg_ref, eu_ref, ed_ref, o_ref, h_ref, gate_ref, acc_ref):
    e = pl.program_id(1)
    tm = x_ref.shape[0]
    lane = lax.broadcasted_iota(I32, (tm, LANES), 1)

    @pl.when(e == 0)
    def _():
        x1, hn = _mix_in(x_ref, yc_ref, at_ref, woa_ref, wob_ref, g_ref)
        hi, lo = _split_hi_lo(hn)
        h_ref[...] = hi
        rhi = rhi_ref[...]
        logits = (jnp.dot(hi, rhi, preferred_element_type=F32)
                  + jnp.dot(lo, rhi, preferred_element_type=F32)
                  + jnp.dot(hi, rlo_ref[...], preferred_element_type=F32))
        logits = jnp.where(lane < N_EXPERTS, logits, -jnp.inf)
        ex = jnp.exp(logits - jnp.max(logits, axis=1, keepdims=True))
        probs = ex / jnp.sum(ex, axis=1, keepdims=True)
        p1 = jnp.max(probs, axis=1, keepdims=True)
        i1 = jnp.min(jnp.where(probs == p1, lane, LANES), axis=1, keepdims=True)
        rest = jnp.where(lane == i1, -1.0, probs)
        p2 = jnp.max(rest, axis=1, keepdims=True)
        i2 = jnp.min(jnp.where(rest == p2, lane, LANES), axis=1, keepdims=True)
        den = p1 + p2
        gate_ref[...] = jnp.where(lane == i1, p1 / den, jnp.where(lane == i2, p2 / den, 0.0))
        acc_ref[...] = x1

    ge = jnp.sum(jnp.where(lane == e, gate_ref[...], 0.0), axis=1, keepdims=True)
    acc_ref[...] += ge * _swiglu(h_ref[...], eg_ref[0], eu_ref[0], ed_ref[0])

    @pl.when(e == pl.num_programs(1) - 1)
    def _():
        o_ref[...] = acc_ref[...]


def _ffn_moe(x, yc, at, lw, tm):
    m, d = x.shape
    ne, _, fe = lw["eg"].shape
    row = lambda n: pl.BlockSpec((tm, n), lambda i, e: (i, 0))
    consts = [lw["wo_a"], lw["wo_b"], lw["ffn_g"], lw["r_hi"], lw["r_lo"]]
    return pl.pallas_call(
        _ffn_moe_kernel, grid=(m // tm, ne),
        in_specs=[row(d), row(CONV_DIM), row(ATTN_DIM)]
        + [pl.BlockSpec(c.shape, lambda i, e: (0, 0), pipeline_mode=pl.Buffered(1)) for c in consts]
        + [pl.BlockSpec((1, d, fe), lambda i, e: (e, 0, 0)),
           pl.BlockSpec((1, d, fe), lambda i, e: (e, 0, 0)),
           pl.BlockSpec((1, fe, d), lambda i, e: (e, 0, 0))],
        out_specs=row(d), out_shape=jax.ShapeDtypeStruct((m, d), F32),
        scratch_shapes=[pltpu.VMEM((tm, d), BF16), pltpu.VMEM((tm, LANES), F32),
                        pltpu.VMEM((tm, d), F32)],
        compiler_params=pltpu.CompilerParams(dimension_semantics=("arbitrary", "arbitrary"),
                                             vmem_limit_bytes=V7X_VMEM_LIMIT),
        name="ffn_moe",
    )(x, yc, at, *consts, lw["eg"], lw["eu"], lw["ed"])


def _block_diag_mean(n, group, valid=None):
    m = np.zeros((n, n), np.float32)
    for s in range(0, n if valid is None else valid, group):
        m[s:s + group, s:s + group] = 1.0 / group
    return jnp.asarray(m, BF16)


def _placement_matrices():
    s1 = np.zeros((N_IDX_HEADS * IDX_DIM, N_IDX_HEADS * LANES), np.float32)
    s2 = np.zeros_like(s1)
    t1 = np.zeros((LANES, LANES), np.float32)
    t2 = np.zeros_like(t1)
    j = np.arange(IDX_DIM)
    for h in range(N_IDX_HEADS):
        s1[h * IDX_DIM + j, h * LANES + j] = 1.0
        s2[h * IDX_DIM + j, h * LANES + IDX_DIM + j] = 1.0
        s1[h * IDX_DIM + j, h * LANES + 2 * IDX_DIM + j] = 1.0
    t1[j, j] = 1.0
    t1[j, IDX_DIM + j] = 1.0
    t2[j, 2 * IDX_DIM + j] = 1.0
    return tuple(jnp.asarray(a, BF16) for a in (s1, s2, t1, t2))


def _layer_weights(l, p):
    d = p["w_in"].shape[1]
    s1, s2, t1, t2 = _placement_matrices()
    lw = {
        "attn_g": p["attn_norm_g"][l][None],
        "w_in": jnp.pad(p["w_in"][l], ((0, 0), (0, IN_COLS_PAD - IN_COLS))).astype(BF16),
        "conv_w": p["conv_w"][l],
        "gq": jnp.tile(p["q_norm_g"][l], N_HEADS)[None],
        "gk": jnp.tile(p["k_norm_g"][l], N_KV)[None],
        "giq": jnp.tile(p["iq_norm_g"][l], N_IDX_HEADS)[None],
        "gik": jnp.pad(p["ik_norm_g"][l], (0, LANES - IDX_DIM))[None],
        "mq": _block_diag_mean(ATTN_DIM, HEAD_DIM),
        "mk": _block_diag_mean(N_KV * HEAD_DIM, HEAD_DIM),
        "miq": _block_diag_mean(N_IDX_HEADS * IDX_DIM, IDX_DIM),
        "mik": _block_diag_mean(LANES, IDX_DIM, valid=IDX_DIM),
        "s1": s1, "s2": s2, "t1": t1, "t2": t2,
        "wo_a": p["w_out"][l][:CONV_DIM].astype(BF16),
        "wo_b": p["w_out"][l][CONV_DIM:].astype(BF16),
        "ffn_g": p["ffn_norm_g"][l][None],
    }
    j = l // 2
    if l % 2 == 0:
        lw.update(wg=p["ffn_w_gate"][j].astype(BF16), wu=p["ffn_w_up"][j].astype(BF16),
                  wd=p["ffn_w_down"][j].astype(BF16))
    else:
        r = jnp.pad(p["router_w"][j], ((0, 0), (0, LANES - N_EXPERTS)))
        r_hi = r.astype(BF16)
        lw.update(r_hi=r_hi, r_lo=(r - r_hi.astype(F32)).astype(BF16),
                  eg=p["moe_w_gate"][j].astype(BF16), eu=p["moe_w_up"][j].astype(BF16),
                  ed=p["moe_w_down"][j].astype(BF16))
    assert d == lw["wo_a"].shape[1]
    return lw


def _pick_tile(n, pref):
    t = min(n, pref)
    while n % t:
        t //= 2
    return t


def _mixer(x, prev, lw, *, tq, past_k2=None, past_v=None, past_ik=None, q_pos0=0):
    b, t, d = x.shape
    yc, q, k, v, k2, vb, iqcat, iw, ik, ikcat, cst = _inproj(x, prev, lw, _pick_tile(t, 512))
    if past_k2 is not None:
        k2 = jnp.concatenate([past_k2, k2], axis=1)
        vb = jnp.concatenate([past_v, vb], axis=1)
        ikcat = jnp.concatenate([past_ik, ikcat], axis=1)
    n_keys = k2.shape[1]
    pad = (-n_keys) % KEY_CHUNK
    if pad:
        k2, vb, ikcat = (jnp.pad(a, ((0, 0), (0, pad), (0, 0))) for a in (k2, vb, ikcat))
    at = _attention(q, iqcat, iw, k2, vb, ikcat, tq=tq, n_keys=n_keys, q_pos0=q_pos0)
    m = b * t
    x2, yc2, at2 = x.reshape(m, d), yc.reshape(m, CONV_DIM), at.reshape(m, ATTN_DIM)
    if "wg" in lw:
        y = _ffn_dense(x2, yc2, at2, lw, _pick_tile(m, 512))
    else:
        y = _ffn_moe(x2, yc2, at2, lw, _pick_tile(m, 512))
    return (y.reshape(b, t, d), k.reshape(b, t, N_KV, HEAD_DIM), v.reshape(b, t, N_KV, HEAD_DIM),
            ik, cst)


def kernel(x_prompt, x_sample, cache_k, cache_v, cache_ik, state_conv, attn_norm_g, w_in, conv_w,
           q_norm_g, k_norm_g, iq_norm_g, ik_norm_g, w_out, ffn_norm_g, ffn_w_gate, ffn_w_up,
           ffn_w_down, router_w, moe_w_gate, moe_w_up, moe_w_down):
    params = dict(attn_norm_g=attn_norm_g, w_in=w_in, conv_w=conv_w, q_norm_g=q_norm_g,
                  k_norm_g=k_norm_g, iq_norm_g=iq_norm_g, ik_norm_g=ik_norm_g, w_out=w_out,
                  ffn_norm_g=ffn_norm_g, ffn_w_gate=ffn_w_gate, ffn_w_up=ffn_w_up,
                  ffn_w_down=ffn_w_down, router_w=router_w, moe_w_gate=moe_w_gate,
                  moe_w_up=moe_w_up, moe_w_down=moe_w_down)
    depth = w_in.shape[0]
    xp, xs = x_prompt, x_sample
    bp = xp.shape[0]
    bs, ts, _ = xs.shape
    past = cache_k.shape[2]
    kvd = N_KV * HEAD_DIM
    outs = [[] for _ in range(8)]
    for l in range(depth):
        lw = _layer_weights(l, params)
        prev0 = jnp.zeros((bp, CONV_W - 1, CONV_DIM), F32)
        xp, kp, vp, ikp, cp = _mixer(xp, prev0, lw, tq=Q_BLOCK)

        ck = cache_k[l].reshape(bs, past, kvd)
        g0, g1 = ck[..., :HEAD_DIM], ck[..., HEAD_DIM:]
        past_k2 = jnp.concatenate([g0, g0, g1, g1], axis=-1).astype(BF16)
        past_v = cache_v[l].reshape(bs, past, kvd).astype(BF16)
        cik = cache_ik[l]
        cik_hi = cik.astype(BF16)
        cik_lo = (cik - cik_hi.astype(F32)).astype(BF16)
        past_ik = jnp.concatenate([cik_hi, cik_hi, cik_lo, jnp.zeros_like(cik_hi)], axis=-1)
        xs, ks, vs, iks, cs = _mixer(xs, state_conv[l], lw, tq=ts, past_k2=past_k2,
                                     past_v=past_v, past_ik=past_ik, q_pos0=past)
        for lst, val in zip(outs, (kp, vp, ikp, cp, ks, vs, iks, cs)):
            lst.append(val)
    kp, vp, ikp, cp, ks, vs, iks, cs = (jnp.stack(o) for o in outs)
    return (xp, xs, kp, vp, ikp, cp, ks, vs, iks, cs)
```

```python
import functools
import math

import numpy as np
import jax
import jax.numpy as jnp
from jax import lax
from jax.experimental import pallas as pl
from jax.experimental.pallas import tpu as pltpu

CHUNK = 64
CONV_DIM = 512
CONV_W = 3
N_HEADS = 8
HEAD_DIM = 64
ATTN_DIM = N_HEADS * HEAD_DIM
N_KV = 2
N_IDX_HEADS = 8
IDX_DIM = 32
TOPK_MAX = 256
Q_BLOCK = 128
N_EXPERTS = 8
EPS = 1e-6
COL_SIZES = (CONV_DIM, CONV_DIM, CONV_DIM, ATTN_DIM, N_KV * HEAD_DIM, N_KV * HEAD_DIM,
             N_IDX_HEADS * IDX_DIM, IDX_DIM, N_IDX_HEADS)
IN_COLS = sum(COL_SIZES)

LANES = 128
V7X_VMEM_LIMIT = 56 * 1024 * 1024

IN_COLS_PAD = ((IN_COLS + LANES - 1) // LANES) * LANES
TAIL_OFF = IN_COLS_PAD - LANES
IW_LANE = IDX_DIM
QSCALE = HEAD_DIM ** -0.5 * math.log2(math.e)
IWSCALE = IDX_DIM ** -0.5 * N_IDX_HEADS ** -0.5
NEG = -1e30
INT_MIN = -2 ** 31
INT_MAX = 2 ** 31 - 1
KEY_CHUNK = 512
BISECT_ROUND = 4
FIRST_PROBE_GAP = 1 << 24

F32 = jnp.float32
BF16 = jnp.bfloat16
I32 = jnp.int32


def _split_hi_lo(x):
    hi = x.astype(BF16)
    lo = (x - hi.astype(F32)).astype(BF16)
    return hi, lo


def _group_rsqrt(x, g_ref):
    hi, lo = _split_hi_lo(x * x)
    g = g_ref[...]
    ms = jnp.dot(hi, g, preferred_element_type=F32) + jnp.dot(lo, g, preferred_element_type=F32)
    return lax.rsqrt(ms + EPS)


def _rmsnorm(x, g):
    return x * lax.rsqrt(jnp.mean(x * x, axis=-1, keepdims=True) + EPS) * g


def _inproj_kernel(x_ref, prev_ref, g_ref, w_ref, cw_ref, gq_ref, gk_ref, giq_ref, gik_ref,
                   mq_ref, mk_ref, miq_ref, mik_ref, s1_ref, s2_ref, t1_ref, t2_ref,
                   yconv_ref, q_ref, k_ref, v_ref, k2_ref, vb_ref, iqcat_ref, iw_ref, ik_ref,
                   ikcat_ref, cst_ref, carry_ref):
    t = pl.program_id(1)
    nt = pl.num_programs(1)
    x = x_ref[0]
    tm = x.shape[0]
    h = _rmsnorm(x, g_ref[...]).astype(BF16)
    p = jnp.dot(h, w_ref[...], preferred_element_type=F32)
    o = np.cumsum((0,) + COL_SIZES)
    c_gate, b_gate, u = p[:, o[0]:o[1]], p[:, o[1]:o[2]], p[:, o[2]:o[3]]
    q, k, v, iq = p[:, o[3]:o[4]], p[:, o[4]:o[5]], p[:, o[5]:o[6]], p[:, o[6]:o[7]]
    tail = p[:, TAIL_OFF:]

    ci = c_gate * u

    @pl.when(t == 0)
    def _():
        carry_ref[8 - (CONV_W - 1):, :] = prev_ref[0]

    row = lax.broadcasted_iota(I32, ci.shape, 0)
    c7 = carry_ref[7:8, :]
    c6 = carry_ref[6:7, :]
    s1 = jnp.where(row == 0, c7, pltpu.roll(ci, 1, axis=0))
    s2 = jnp.where(row == 0, c6, jnp.where(row == 1, c7, pltpu.roll(ci, 2, axis=0)))
    cw = cw_ref[...]
    conv = cw[0:1] * s2 + cw[1:2] * s1 + cw[2:3] * ci
    yconv_ref[0] = (b_gate * conv).astype(BF16)
    carry_ref[...] = ci[tm - 8:, :]

    @pl.when(t == nt - 1)
    def _():
        cst_ref[0] = ci[tm - (CONV_W - 1):, :]

    qn = q * _group_rsqrt(q, mq_ref) * gq_ref[...]
    q_ref[0] = (qn * QSCALE).astype(BF16)
    kn = k * _group_rsqrt(k, mk_ref) * gk_ref[...]
    k_ref[0] = kn
    v_ref[0] = v
    lane = lax.broadcasted_iota(I32, kn.shape, 1)
    kr = pltpu.roll(kn, HEAD_DIM, axis=1)
    k2_ref[0, :, 0:LANES] = jnp.where(lane < HEAD_DIM, kn, kr).astype(BF16)
    k2_ref[0, :, LANES:2 * LANES] = jnp.where(lane < HEAD_DIM, kr, kn).astype(BF16)
    vb_ref[0] = v.astype(BF16)

    iqn = iq * _group_rsqrt(iq, miq_ref) * giq_ref[...]
    ihi, ilo = _split_hi_lo(iqn)
    iqcat_ref[0] = (jnp.dot(ihi, s1_ref[...], preferred_element_type=F32)
                    + jnp.dot(ilo, s2_ref[...], preferred_element_type=F32)).astype(BF16)
    tn = tail * _group_rsqrt(tail, mik_ref) * gik_ref[...]
    ik_ref[0] = tn[:, 0:IDX_DIM]
    thi, tlo = _split_hi_lo(tn)
    ikcat_ref[0] = (jnp.dot(thi, t1_ref[...], preferred_element_type=F32)
                    + jnp.dot(tlo, t2_ref[...], preferred_element_type=F32)).astype(BF16)
    iw_ref[0] = tail * IWSCALE


def _const_spec(shape):
    nd = len(shape)
    return pl.BlockSpec(shape, lambda *_: (0,) * nd)


def _inproj(x, prev, lw, tm):
    b, t, d = x.shape
    grid = (b, t // tm)
    tile = lambda n: pl.BlockSpec((1, tm, n), lambda i, j: (i, j, 0))
    consts = [lw["attn_g"], lw["w_in"], lw["conv_w"], lw["gq"], lw["gk"], lw["giq"], lw["gik"],
              lw["mq"], lw["mk"], lw["miq"], lw["mik"], lw["s1"], lw["s2"], lw["t1"], lw["t2"]]
    in_specs = ([tile(d), pl.BlockSpec((1, CONV_W - 1, CONV_DIM), lambda i, j: (i, 0, 0))]
                + [_const_spec(c.shape) for c in consts])
    kvd = N_KV * HEAD_DIM
    out_shape = (
        jax.ShapeDtypeStruct((b, t, CONV_DIM), BF16),
        jax.ShapeDtypeStruct((b, t, ATTN_DIM), BF16),
        jax.ShapeDtypeStruct((b, t, kvd), F32),
        jax.ShapeDtypeStruct((b, t, kvd), F32),
        jax.ShapeDtypeStruct((b, t, 2 * kvd), BF16),
        jax.ShapeDtypeStruct((b, t, kvd), BF16),
        jax.ShapeDtypeStruct((b, t, N_IDX_HEADS * LANES), BF16),
        jax.ShapeDtypeStruct((b, t, LANES), F32),
        jax.ShapeDtypeStruct((b, t, IDX_DIM), F32),
        jax.ShapeDtypeStruct((b, t, LANES), BF16),
        jax.ShapeDtypeStruct((b, CONV_W - 1, CONV_DIM), F32),
    )
    out_specs = (tile(CONV_DIM), tile(ATTN_DIM), tile(kvd), tile(kvd), tile(2 * kvd), tile(kvd),
                 tile(N_IDX_HEADS * LANES), tile(LANES), tile(IDX_DIM), tile(LANES),
                 pl.BlockSpec((1, CONV_W - 1, CONV_DIM), lambda i, j: (i, 0, 0)))
    return pl.pallas_call(
        _inproj_kernel, grid=grid, in_specs=in_specs, out_specs=out_specs, out_shape=out_shape,
        scratch_shapes=[pltpu.VMEM((8, CONV_DIM), F32)],
        compiler_params=pltpu.CompilerParams(dimension_semantics=("arbitrary", "arbitrary"),
                                             vmem_limit_bytes=V7X_VMEM_LIMIT),
        name="inproj",
    )(x, prev, *consts)


def _slabs(n):
    return [slice(j * LANES, (j + 1) * LANES) for j in range(n // LANES)]


def _tree(op, xs):
    while len(xs) > 1:
        xs = [op(xs[i], xs[i + 1]) for i in range(0, len(xs) - 1, 2)] + ([xs[-1]] if len(xs) % 2 else [])
    return xs[0]


def _attn_kernel(q_ref, iqcat_ref, iw_ref, k2_ref, v_ref, ikcat_ref, o_ref,
                 keyst_ref, keys_ref, lga_ref, lgb_ref, lhs_ref, wb_ref, qg_ref, m_ref, l_ref, acc_ref,
                 *, tq, kc, n_keys, q_pos0, topk):
    qi = pl.program_id(1)
    q0 = q_pos0 + qi * tq
    shift = CHUNK.bit_length() - 1

    def key_limit(pos):
        return jnp.minimum(((pos >> shift) + 1) << shift, n_keys)

    lane = lax.broadcasted_iota(I32, (tq, LANES), 1)
    lim = key_limit(q0 + lax.broadcasted_iota(I32, (tq, LANES), 0))
    qlane = lax.broadcasted_iota(I32, (1, LANES), 1)
    lim_t = jnp.where(qlane < tq, key_limit(q0 + qlane), 0)
    n_chunks = (key_limit(q0 + tq - 1) + kc - 1) // kc
    kfl = jnp.minimum(lim_t, topk).astype(F32)

    iqc = iqcat_ref[0]
    for h in range(N_IDX_HEADS):
        lhs_ref[h * tq:(h + 1) * tq, :] = iqc[:, h * LANES:(h + 1) * LANES]
    iw = iw_ref[0]
    for h in range(N_IDX_HEADS):
        col = jnp.sum(jnp.where(lane == IW_LANE + h, iw, 0.0), axis=1, keepdims=True)
        wb_ref[h] = jnp.broadcast_to(col, (tq, LANES))
    pad_rows = jnp.full((LANES - tq, LANES), INT_MIN, I32) if tq < LANES else None

    last_chunk = keys_ref.shape[0] - 1

    def logits(c, lg_ref):
        off = pl.multiple_of(jnp.minimum(c, last_chunk) * kc, kc)
        lg_ref[...] = lax.dot_general(lhs_ref[...], ikcat_ref[0, pl.ds(off, kc), :],
                                      (((1,), (1,)), ((), ())), preferred_element_type=F32)

    def keys_from(c, lg_ref, kmax):
        c = jnp.minimum(c, last_chunk)
        for j, sl in enumerate(_slabs(kc)):
            sc = None
            for h in range(N_IDX_HEADS):
                term = jnp.maximum(lg_ref[h * tq:(h + 1) * tq, sl], 0.0) * wb_ref[h]
                sc = term if sc is None else sc + term
            sc = sc + 0.0
            bits = lax.bitcast_convert_type(sc, I32)
            key = bits ^ ((bits >> 31) & INT_MAX)
            key = jnp.where(c * kc + j * LANES + lane < lim, key, INT_MIN)
            keys_ref[c, :, sl] = key
            if pad_rows is not None:
                key = jnp.concatenate([key, pad_rows], axis=0)
            kt = key.T
            keyst_ref[c, sl, :] = kt
            kmax = jnp.maximum(kmax, _tree(jnp.maximum, [kt[8 * i:8 * i + 8, :]
                                                         for i in range(LANES // 8)]))
        return kmax

    logits(0, lga_ref)

    def score_pair(i, kmax):
        logits(2 * i + 1, lgb_ref)
        kmax = keys_from(2 * i, lga_ref, kmax)
        logits(2 * i + 2, lga_ref)
        return keys_from(2 * i + 1, lgb_ref, kmax)

    kmax = lax.fori_loop(0, (n_chunks + 1) // 2, score_pair, jnp.full((8, LANES), INT_MIN, I32))
    kmax = jnp.max(kmax, axis=0, keepdims=True)

    tree_sum = functools.partial(_tree, jnp.add)

    def count_where(pred):
        sub = lax.broadcasted_iota(I32, (kc, LANES), 0)

        def body(c, acc):
            hit = jnp.where(pred(keyst_ref[c], c * kc + sub), 1.0, 0.0)
            return acc + tree_sum([hit[8 * i:8 * i + 8, :] for i in range(kc // 8)])

        part = lax.fori_loop(0, n_chunks, body, jnp.zeros((8, LANES), F32))
        return jnp.sum(part, axis=0, keepdims=True)

    def bisect_cond(carry):
        it, pending = carry[0], carry[1]
        return jnp.logical_and(pending > 0.0, it < 12)

    def midpoint(lo, hi):
        return (lo & hi) + ((lo ^ hi) >> 1)

    def bisect_pass(lo, hi, clo, chi, mid=None):
        mid = midpoint(lo, hi) if mid is None else mid
        cnt = count_where(lambda kch, kpos: kch >= mid)
        ge = cnt >= kfl
        exact = cnt == kfl
        lo_n = jnp.where(ge, mid, lo)
        hi_n = jnp.where(exact, mid + 1, jnp.where(ge, hi, mid))
        return lo_n, hi_n, jnp.where(ge, cnt, clo), jnp.where(ge, chi, cnt)

    def bisect_body(carry):
        it, _, lo, hi, clo, chi = carry
        for _ in range(BISECT_ROUND):
            lo, hi, clo, chi = bisect_pass(lo, hi, clo, chi)
        pending = jnp.max(jnp.where(hi != lo + 1, 1.0, 0.0), axis=1, keepdims=True)[0, 0]
        return (it + 1, pending, lo, hi, clo, chi)

    lo = jnp.full((1, LANES), INT_MIN + 1, I32)
    hi = jnp.maximum(jnp.minimum(kmax, INT_MAX - 1) + 1, lo + 1)
    probe = jnp.where(kmax > INT_MIN + 2 + FIRST_PROBE_GAP, kmax - FIRST_PROBE_GAP, midpoint(lo, hi))
    state = bisect_pass(lo, hi, lim_t.astype(F32), jnp.zeros((1, LANES), F32), probe)
    _, _, thr, _, clo, chi = lax.while_loop(bisect_cond, bisect_body,
                                            (jnp.int32(0), jnp.float32(1.0)) + state)

    extra = clo - kfl
    any_tie = jnp.max(extra, axis=1, keepdims=True)[0, 0]

    @pl.when(any_tie > 0.0)
    def _():
        need = kfl - chi

        def pos_body(_, carry):
            plo, phi = carry
            mid = (plo + phi) >> 1
            cnt = count_where(lambda kch, kpos: jnp.logical_and(kch == thr, kpos <= mid))
            ok = cnt >= need
            return jnp.where(ok, plo, mid), jnp.where(ok, mid, phi)

        n_steps = int(n_keys).bit_length() + 1
        _, last = lax.fori_loop(0, n_steps, pos_body,
                                (jnp.full((1, LANES), -1, I32),
                                 jnp.full((1, LANES), keyst_ref.shape[0] * kc - 1, I32)))
        last = jnp.where(extra > 0.0, last, INT_MAX)
        sub = lax.broadcasted_iota(I32, (kc, LANES), 0)

        def demote(c, carry):
            kch = keyst_ref[c]
            drop = jnp.logical_and(kch == thr, c * kc + sub > last)
            kch = jnp.where(drop, kch - 1, kch)
            keyst_ref[c] = kch
            for sl in _slabs(kc):
                keys_ref[c, :, sl] = kch[sl, :].T[:tq, :]
            return carry

        lax.fori_loop(0, n_chunks, demote, 0)

    thr_rows = jnp.broadcast_to(thr, (LANES, LANES)).T[:tq, :]
    qv = q_ref[0].astype(F32)
    heads_per_group = N_HEADS // N_KV
    rows = heads_per_group * tq
    for g in range(N_KV):
        for r in range(heads_per_group):
            hd = g * heads_per_group + r
            slab = qv[:, (hd // 2) * LANES:(hd // 2 + 1) * LANES]
            keep = (lane < HEAD_DIM) if hd % 2 == 0 else (lane >= HEAD_DIM)
            qg_ref[g, r * tq:(r + 1) * tq, :] = jnp.where(keep, slab, 0.0).astype(BF16)
    m_ref[...] = jnp.full(m_ref.shape, NEG, F32)
    l_ref[...] = jnp.zeros(l_ref.shape, F32)
    acc_ref[...] = jnp.zeros(acc_ref.shape, F32)

    def attend(c, carry):
        off = pl.multiple_of(c * kc, kc)
        keys = keys_ref[c]
        bias = [jnp.where(keys[:, sl] >= thr_rows, 0.0, NEG) for sl in _slabs(kc)]
        bias = [jnp.concatenate([b] * heads_per_group, axis=0) for b in bias]
        vch = v_ref[0, pl.ds(off, kc), :]
        for g in range(N_KV):
            kch = k2_ref[0, pl.ds(off, kc), g * LANES:(g + 1) * LANES]
            s = lax.dot_general(qg_ref[g], kch, (((1,), (1,)), ((), ())),
                                preferred_element_type=F32)
            sm = [s[:, sl] + b for sl, b in zip(_slabs(kc), bias)]
            part = sm[0]
            for piece in sm[1:]:
                part = jnp.maximum(part, piece)
            m_prev = m_ref[g]
            m_new = jnp.maximum(m_prev, jnp.max(part, axis=1, keepdims=True))
            alpha = jnp.exp2(m_prev - m_new)
            ps = [jnp.exp2(piece - m_new) for piece in sm]
            p = jnp.concatenate(ps, axis=1).astype(BF16)
            l_ref[g] = alpha * l_ref[g] + tree_sum(ps)
            acc_ref[g] = alpha * acc_ref[g] + jnp.dot(p, vch, preferred_element_type=F32)
            m_ref[g] = m_new
        return carry

    lax.fori_loop(0, n_chunks, attend, 0)

    for g in range(N_KV):
        out = acc_ref[g] / jnp.sum(l_ref[g], axis=1, keepdims=True)
        for jj in range(heads_per_group // 2):
            a = out[(2 * jj) * tq:(2 * jj + 1) * tq, :]
            bb = out[(2 * jj + 1) * tq:(2 * jj + 2) * tq, :]
            if g % 2 == 0:
                slab = jnp.where(lane < HEAD_DIM, a, pltpu.roll(bb, HEAD_DIM, axis=1))
            else:
                slab = jnp.where(lane < HEAD_DIM, pltpu.roll(a, HEAD_DIM, axis=1), bb)
            so = (g * heads_per_group // 2 + jj) * LANES
            o_ref[0, :, so:so + LANES] = slab.astype(BF16)


def _attention(q, iqcat, iw, k2, vb, ikcat, *, tq, n_keys, q_pos0):
    b, t, _ = q.shape
    lpad = k2.shape[1]
    kc = KEY_CHUNK
    assert lpad % kc == 0 and t % tq == 0 and tq <= LANES and CHUNK & (CHUNK - 1) == 0
    topk = min(TOPK_MAX, n_keys // 4)
    kern = functools.partial(_attn_kernel, tq=tq, kc=kc, n_keys=n_keys, q_pos0=q_pos0, topk=topk)
    qtile = lambda n: pl.BlockSpec((1, tq, n), lambda i, j: (i, j, 0))
    ktile = lambda n: pl.BlockSpec((1, lpad, n), lambda i, j: (i, 0, 0))
    rows = (N_HEADS // N_KV) * tq
    return pl.pallas_call(
        kern, grid=(b, t // tq),
        in_specs=[qtile(ATTN_DIM), qtile(N_IDX_HEADS * LANES), qtile(LANES),
                  ktile(2 * N_KV * HEAD_DIM), ktile(N_KV * HEAD_DIM), ktile(LANES)],
        out_specs=qtile(ATTN_DIM),
        out_shape=jax.ShapeDtypeStruct((b, t, ATTN_DIM), BF16),
        scratch_shapes=[
            pltpu.VMEM((lpad // kc, kc, LANES), I32),
            pltpu.VMEM((lpad // kc, tq, kc), I32),
            pltpu.VMEM((N_IDX_HEADS * tq, kc), F32),
            pltpu.VMEM((N_IDX_HEADS * tq, kc), F32),
            pltpu.VMEM((N_IDX_HEADS * tq, LANES), BF16),
            pltpu.VMEM((N_IDX_HEADS, tq, LANES), F32),
            pltpu.VMEM((N_KV, rows, LANES), BF16),
            pltpu.VMEM((N_KV, rows, LANES), F32),
            pltpu.VMEM((N_KV, rows, LANES), F32),
            pltpu.VMEM((N_KV, rows, LANES), F32),
        ],
        compiler_params=pltpu.CompilerParams(dimension_semantics=("arbitrary", "arbitrary"),
                                             vmem_limit_bytes=V7X_VMEM_LIMIT),
        name="attention",
    )(q, iqcat, iw, k2, vb, ikcat)


def _mix_in(x_ref, yc_ref, at_ref, woa_ref, wob_ref, g_ref):
    x1 = (x_ref[...] + jnp.dot(yc_ref[...], woa_ref[...], preferred_element_type=F32)
          + jnp.dot(at_ref[...], wob_ref[...], preferred_element_type=F32))
    return x1, _rmsnorm(x1, g_ref[...])


def _swiglu(hb, wg, wu, wd):
    a = jnp.dot(hb, wg, preferred_element_type=F32)
    b = jnp.dot(hb, wu, preferred_element_type=F32)
    act = (a * (1.0 / (1.0 + jnp.exp(-a))) * b).astype(BF16)
    return jnp.dot(act, wd, preferred_element_type=F32)


def _ffn_dense_kernel(x_ref, yc_ref, at_ref, woa_ref, wob_ref, g_ref, wg_ref, wu_ref, wd_ref, o_ref):
    x1, hn = _mix_in(x_ref, yc_ref, at_ref, woa_ref, wob_ref, g_ref)
    o_ref[...] = x1 + _swiglu(hn.astype(BF16), wg_ref[...], wu_ref[...], wd_ref[...])


def _ffn_dense(x, yc, at, lw, tm):
    m, d = x.shape
    row = lambda n: pl.BlockSpec((tm, n), lambda i: (i, 0))
    consts = [lw["wo_a"], lw["wo_b"], lw["ffn_g"], lw["wg"], lw["wu"], lw["wd"]]
    return pl.pallas_call(
        _ffn_dense_kernel, grid=(m // tm,),
        in_specs=[row(d), row(CONV_DIM), row(ATTN_DIM)]
        + [pl.BlockSpec(c.shape, lambda i: (0, 0), pipeline_mode=pl.Buffered(1)) for c in consts],
        out_specs=row(d), out_shape=jax.ShapeDtypeStruct((m, d), F32),
        compiler_params=pltpu.CompilerParams(dimension_semantics=("arbitrary",),
                                             vmem_limit_bytes=V7X_VMEM_LIMIT),
        name="ffn_dense",
    )(x, yc, at, *consts)


def _ffn_moe_kernel(x_ref, yc_ref, at_ref, woa_ref, wob_ref, g_ref, rhi_ref, rlo_ref,
                    eg_ref, eu_ref, ed_ref, o_ref, h_ref, gate_ref, acc_ref):
    e = pl.program_id(1)
    tm = x_ref.shape[0]
    lane = lax.broadcasted_iota(I32, (tm, LANES), 1)

    @pl.when(e == 0)
    def _():
        x1, hn = _mix_in(x_ref, yc_ref, at_ref, woa_ref, wob_ref, g_ref)
        hi, lo = _split_hi_lo(hn)
        h_ref[...] = hi
        rhi = rhi_ref[...]
        logits = (jnp.dot(hi, rhi, preferred_element_type=F32)
                  + jnp.dot(lo, rhi, preferred_element_type=F32)
                  + jnp.dot(hi, rlo_ref[...], preferred_element_type=F32))
        logits = jnp.where(lane < N_EXPERTS, logits, -jnp.inf)
        ex = jnp.exp(logits - jnp.max(logits, axis=1, keepdims=True))
        probs = ex / jnp.sum(ex, axis=1, keepdims=True)
        p1 = jnp.max(probs, axis=1, keepdims=True)
        i1 = jnp.min(jnp.where(probs == p1, lane, LANES), axis=1, keepdims=True)
        rest = jnp.where(lane == i1, -1.0, probs)
        p2 = jnp.max(rest, axis=1, keepdims=True)
        i2 = jnp.min(jnp.where(rest == p2, lane, LANES), axis=1, keepdims=True)
        den = p1 + p2
        gate_ref[...] = jnp.where(lane == i1, p1 / den, jnp.where(lane == i2, p2 / den, 0.0))
        acc_ref[...] = x1

    ge = jnp.sum(jnp.where(lane == e, gate_ref[...], 0.0), axis=1, keepdims=True)
    acc_ref[...] += ge * _swiglu(h_ref[...], eg_ref[0], eu_ref[0], ed_ref[0])

    @pl.when(e == pl.num_programs(1) - 1)
    def _():
        o_ref[...] = acc_ref[...]


def _ffn_moe(x, yc, at, lw, tm):
    m, d = x.shape
    ne, _, fe = lw["eg"].shape
    row = lambda n: pl.BlockSpec((tm, n), lambda i, e: (i, 0))
    consts = [lw["wo_a"], lw["wo_b"], lw["ffn_g"], lw["r_hi"], lw["r_lo"]]
    return pl.pallas_call(
        _ffn_moe_kernel, grid=(m // tm, ne),
        in_specs=[row(d), row(CONV_DIM), row(ATTN_DIM)]
        + [pl.BlockSpec(c.shape, lambda i, e: (0, 0), pipeline_mode=pl.Buffered(1)) for c in consts]
        + [pl.BlockSpec((1, d, fe), lambda i, e: (e, 0, 0)),
           pl.BlockSpec((1, d, fe), lambda i, e: (e, 0, 0)),
           pl.BlockSpec((1, fe, d), lambda i, e: (e, 0, 0))],
        out_specs=row(d), out_shape=jax.ShapeDtypeStruct((m, d), F32),
        scratch_shapes=[pltpu.VMEM((tm, d), BF16), pltpu.VMEM((tm, LANES), F32),
                        pltpu.VMEM((tm, d), F32)],
        compiler_params=pltpu.CompilerParams(dimension_semantics=("arbitrary", "arbitrary"),
                                             vmem_limit_bytes=V7X_VMEM_LIMIT),
        name="ffn_moe",
    )(x, yc, at, *consts, lw["eg"], lw["eu"], lw["ed"])


def _block_diag_mean(n, group, valid=None):
    m = np.zeros((n, n), np.float32)
    for s in range(0, n if valid is None else valid, group):
        m[s:s + group, s:s + group] = 1.0 / group
    return jnp.asarray(m, BF16)


def _placement_matrices():
    s1 = np.zeros((N_IDX_HEADS * IDX_DIM, N_IDX_HEADS * LANES), np.float32)
    s2 = np.zeros_like(s1)
    t1 = np.zeros((LANES, LANES), np.float32)
    t2 = np.zeros_like(t1)
    j = np.arange(IDX_DIM)
    for h in range(N_IDX_HEADS):
        s1[h * IDX_DIM + j, h * LANES + j] = 1.0
        s2[h * IDX_DIM + j, h * LANES + IDX_DIM + j] = 1.0
        s1[h * IDX_DIM + j, h * LANES + 2 * IDX_DIM + j] = 1.0
    t1[j, j] = 1.0
    t1[j, IDX_DIM + j] = 1.0
    t2[j, 2 * IDX_DIM + j] = 1.0
    return tuple(jnp.asarray(a, BF16) for a in (s1, s2, t1, t2))


def _layer_weights(l, p):
    d = p["w_in"].shape[1]
    s1, s2, t1, t2 = _placement_matrices()
    lw = {
        "attn_g": p["attn_norm_g"][l][None],
        "w_in": jnp.pad(p["w_in"][l], ((0, 0), (0, IN_COLS_PAD - IN_COLS))).astype(BF16),
        "conv_w": p["conv_w"][l],
        "gq": jnp.tile(p["q_norm_g"][l], N_HEADS)[None],
        "gk": jnp.tile(p["k_norm_g"][l], N_KV)[None],
        "giq": jnp.tile(p["iq_norm_g"][l], N_IDX_HEADS)[None],
        "gik": jnp.pad(p["ik_norm_g"][l], (0, LANES - IDX_DIM))[None],
        "mq": _block_diag_mean(ATTN_DIM, HEAD_DIM),
        "mk": _block_diag_mean(N_KV * HEAD_DIM, HEAD_DIM),
        "miq": _block_diag_mean(N_IDX_HEADS * IDX_DIM, IDX_DIM),
        "mik": _block_diag_mean(LANES, IDX_DIM, valid=IDX_DIM),
        "s1": s1, "s2": s2, "t1": t1, "t2": t2,
        "wo_a": p["w_out"][l][:CONV_DIM].astype(BF16),
        "wo_b": p["w_out"][l][CONV_DIM:].astype(BF16),
        "ffn_g": p["ffn_norm_g"][l][None],
    }
    j = l // 2
    if l % 2 == 0:
        lw.update(wg=p["ffn_w_gate"][j].astype(BF16), wu=p["ffn_w_up"][j].astype(BF16),
                  wd=p["ffn_w_down"][j].astype(BF16))
    else:
        r = jnp.pad(p["router_w"][j], ((0, 0), (0, LANES - N_EXPERTS)))
        r_hi = r.astype(BF16)
        lw.update(r_hi=r_hi, r_lo=(r - r_hi.astype(F32)).astype(BF16),
                  eg=p["moe_w_gate"][j].astype(BF16), eu=p["moe_w_up"][j].astype(BF16),
                  ed=p["moe_w_down"][j].astype(BF16))
    assert d == lw["wo_a"].shape[1]
    return lw


def _pick_tile(n, pref):
    t = min(n, pref)
    while n % t:
        t //= 2
    return t


def _mixer(x, prev, lw, *, tq, past_k2=None, past_v=None, past_ik=None, q_pos0=0):
    b, t, d = x.shape
    yc, q, k, v, k2, vb, iqcat, iw, ik, ikcat, cst = _inproj(x, prev, lw, _pick_tile(t, 512))
    if past_k2 is not None:
        k2 = jnp.concatenate([past_k2, k2], axis=1)
        vb = jnp.concatenate([past_v, vb], axis=1)
        ikcat = jnp.concatenate([past_ik, ikcat], axis=1)
    n_keys = k2.shape[1]
    pad = (-n_keys) % KEY_CHUNK
    if pad:
        k2, vb, ikcat = (jnp.pad(a, ((0, 0), (0, pad), (0, 0))) for a in (k2, vb, ikcat))
    at = _attention(q, iqcat, iw, k2, vb, ikcat, tq=tq, n_keys=n_keys, q_pos0=q_pos0)
    m = b * t
    x2, yc2, at2 = x.reshape(m, d), yc.reshape(m, CONV_DIM), at.reshape(m, ATTN_DIM)
    if "wg" in lw:
        y = _ffn_dense(x2, yc2, at2, lw, _pick_tile(m, 512))
    else:
        y = _ffn_moe(x2, yc2, at2, lw, _pick_tile(m, 512))
    return (y.reshape(b, t, d), k.reshape(b, t, N_KV, HEAD_DIM), v.reshape(b, t, N_KV, HEAD_DIM),
            ik, cst)


def kernel(x_prompt, x_sample, cache_k, cache_v, cache_ik, state_conv, attn_norm_g, w_in, conv_w,
           q_norm_g, k_norm_g, iq_norm_g, ik_norm_g, w_out, ffn_norm_g, ffn_w_gate, ffn_w_up,
           ffn_w_down, router_w, moe_w_gate, moe_w_up, moe_w_down):
    params = dict(attn_norm_g=attn_norm_g, w_in=w_in, conv_w=conv_w, q_norm_g=q_norm_g,
                  k_norm_g=k_norm_g, iq_norm_g=iq_norm_g, ik_norm_g=ik_norm_g, w_out=w_out,
                  ffn_norm_g=ffn_norm_g, ffn_w_gate=ffn_w_gate, ffn_w_up=ffn_w_up,
                  ffn_w_down=ffn_w_down, router_w=router_w, moe_w_gate=moe_w_gate,
                  moe_w_up=moe_w_up, moe_w_down=moe_w_down)
    depth = w_in.shape[0]
    xp, xs = x_prompt, x_sample
    bp = xp.shape[0]
    bs, ts, _ = xs.shape
    past = cache_k.shape[2]
    kvd = N_KV * HEAD_DIM
    outs = [[] for _ in range(8)]
    for l in range(depth):
        lw = _layer_weights(l, params)
        prev0 = jnp.zeros((bp, CONV_W - 1, CONV_DIM), F32)
        xp, kp, vp, ikp, cp = _mixer(xp, prev0, lw, tq=Q_BLOCK)

        ck = cache_k[l].reshape(bs, past, kvd)
        g0, g1 = ck[..., :HEAD_DIM], ck[..., HEAD_DIM:]
        past_k2 = jnp.concatenate([g0, g0, g1, g1], axis=-1).astype(BF16)
        past_v = cache_v[l].reshape(bs, past, kvd).astype(BF16)
        cik = cache_ik[l]
        cik_hi = cik.astype(BF16)
        cik_lo = (cik - cik_hi.astype(F32)).astype(BF16)
        past_ik = jnp.concatenate([cik_hi, cik_hi, cik_lo, jnp.zeros_like(cik_hi)], axis=-1)
        xs, ks, vs, iks, cs = _mixer(xs, state_conv[l], lw, tq=ts, past_k2=past_k2,
                                     past_v=past_v, past_ik=past_ik, q_pos0=past)
        for lst, val in zip(outs, (kp, vp, ikp, cp, ks, vs, iks, cs)):
            lst.append(val)
    kp, vp, ikp, cp, ks, vs, iks, cs = (jnp.stack(o) for o in outs)
    return (xp, xs, kp, vp, ikp, cp, ks, vs, iks, cs)
```

```python
import functools
import math

import numpy as np
import jax
import jax.numpy as jnp
from jax import lax
from jax.experimental import pallas as pl
from jax.experimental.pallas import tpu as pltpu

CHUNK = 64
CONV_DIM = 512
CONV_W = 3
N_HEADS = 8
HEAD_DIM = 64
ATTN_DIM = N_HEADS * HEAD_DIM
N_KV = 2
N_IDX_HEADS = 8
IDX_DIM = 32
TOPK_MAX = 256
Q_BLOCK = 128
N_EXPERTS = 8
EPS = 1e-6
COL_SIZES = (CONV_DIM, CONV_DIM, CONV_DIM, ATTN_DIM, N_KV * HEAD_DIM, N_KV * HEAD_DIM,
             N_IDX_HEADS * IDX_DIM, IDX_DIM, N_IDX_HEADS)
IN_COLS = sum(COL_SIZES)

LANES = 128
V7X_VMEM_LIMIT = 56 * 1024 * 1024

IN_COLS_PAD = ((IN_COLS + LANES - 1) // LANES) * LANES
TAIL_OFF = IN_COLS_PAD - LANES
IW_LANE = IDX_DIM
QSCALE = HEAD_DIM ** -0.5 * math.log2(math.e)
IWSCALE = IDX_DIM ** -0.5 * N_IDX_HEADS ** -0.5
NEG = -1e30
INT_MIN = -2 ** 31
INT_MAX = 2 ** 31 - 1
KEY_CHUNK = 512
BISECT_ROUND = 4
FIRST_PROBE_GAP = 1 << 8

F32 = jnp.float32
BF16 = jnp.bfloat16
I32 = jnp.int32
I16 = jnp.int16


def _split_hi_lo(x):
    hi = x.astype(BF16)
    lo = (x - hi.astype(F32)).astype(BF16)
    return hi, lo


def _group_rsqrt(x, g_ref):
    hi, lo = _split_hi_lo(x * x)
    g = g_ref[...]
    ms = jnp.dot(hi, g, preferred_element_type=F32) + jnp.dot(lo, g, preferred_element_type=F32)
    return lax.rsqrt(ms + EPS)


def _rmsnorm(x, g):
    return x * lax.rsqrt(jnp.mean(x * x, axis=-1, keepdims=True) + EPS) * g


def _inproj_kernel(x_ref, prev_ref, g_ref, w_ref, cw_ref, gq_ref, gk_ref, giq_ref, gik_ref,
                   mq_ref, mk_ref, miq_ref, mik_ref, s1_ref, s2_ref, t1_ref, t2_ref,
                   yconv_ref, q_ref, k_ref, v_ref, k2_ref, vb_ref, iqcat_ref, iw_ref, ik_ref,
                   ikcat_ref, cst_ref, carry_ref):
    t = pl.program_id(1)
    nt = pl.num_programs(1)
    x = x_ref[0]
    tm = x.shape[0]
    h = _rmsnorm(x, g_ref[...]).astype(BF16)
    p = jnp.dot(h, w_ref[...], preferred_element_type=F32)
    o = np.cumsum((0,) + COL_SIZES)
    c_gate, b_gate, u = p[:, o[0]:o[1]], p[:, o[1]:o[2]], p[:, o[2]:o[3]]
    q, k, v, iq = p[:, o[3]:o[4]], p[:, o[4]:o[5]], p[:, o[5]:o[6]], p[:, o[6]:o[7]]
    tail = p[:, TAIL_OFF:]

    ci = c_gate * u

    @pl.when(t == 0)
    def _():
        carry_ref[8 - (CONV_W - 1):, :] = prev_ref[0]

    row = lax.broadcasted_iota(I32, ci.shape, 0)
    c7 = carry_ref[7:8, :]
    c6 = carry_ref[6:7, :]
    s1 = jnp.where(row == 0, c7, pltpu.roll(ci, 1, axis=0))
    s2 = jnp.where(row == 0, c6, jnp.where(row == 1, c7, pltpu.roll(ci, 2, axis=0)))
    cw = cw_ref[...]
    conv = cw[0:1] * s2 + cw[1:2] * s1 + cw[2:3] * ci
    yconv_ref[0] = (b_gate * conv).astype(BF16)
    carry_ref[...] = ci[tm - 8:, :]

    @pl.when(t == nt - 1)
    def _():
        cst_ref[0] = ci[tm - (CONV_W - 1):, :]

    qn = q * _group_rsqrt(q, mq_ref) * gq_ref[...]
    q_ref[0] = (qn * QSCALE).astype(BF16)
    kn = k * _group_rsqrt(k, mk_ref) * gk_ref[...]
    k_ref[0] = kn
    v_ref[0] = v
    lane = lax.broadcasted_iota(I32, kn.shape, 1)
    kr = pltpu.roll(kn, HEAD_DIM, axis=1)
    k2_ref[0, :, 0:LANES] = jnp.where(lane < HEAD_DIM, kn, kr).astype(BF16)
    k2_ref[0, :, LANES:2 * LANES] = jnp.where(lane < HEAD_DIM, kr, kn).astype(BF16)
    vb_ref[0] = v.astype(BF16)

    iqn = iq * _group_rsqrt(iq, miq_ref) * giq_ref[...]
    ihi, ilo = _split_hi_lo(iqn)
    iqcat_ref[0] = (jnp.dot(ihi, s1_ref[...], preferred_element_type=F32)
                    + jnp.dot(ilo, s2_ref[...], preferred_element_type=F32)).astype(BF16)
    tn = tail * _group_rsqrt(tail, mik_ref) * gik_ref[...]
    ik_ref[0] = tn[:, 0:IDX_DIM]
    thi, tlo = _split_hi_lo(tn)
    ikcat_ref[0] = (jnp.dot(thi, t1_ref[...], preferred_element_type=F32)
                    + jnp.dot(tlo, t2_ref[...], preferred_element_type=F32)).astype(BF16)
    iw_ref[0] = tail * IWSCALE


def _const_spec(shape):
    nd = len(shape)
    return pl.BlockSpec(shape, lambda *_: (0,) * nd)


def _inproj(x, prev, lw, tm):
    b, t, d = x.shape
    grid = (b, t // tm)
    tile = lambda n: pl.BlockSpec((1, tm, n), lambda i, j: (i, j, 0))
    consts = [lw["attn_g"], lw["w_in"], lw["conv_w"], lw["gq"], lw["gk"], lw["giq"], lw["gik"],
              lw["mq"], lw["mk"], lw["miq"], lw["mik"], lw["s1"], lw["s2"], lw["t1"], lw["t2"]]
    in_specs = ([tile(d), pl.BlockSpec((1, CONV_W - 1, CONV_DIM), lambda i, j: (i, 0, 0))]
                + [_const_spec(c.shape) for c in consts])
    kvd = N_KV * HEAD_DIM
    out_shape = (
        jax.ShapeDtypeStruct((b, t, CONV_DIM), BF16),
        jax.ShapeDtypeStruct((b, t, ATTN_DIM), BF16),
        jax.ShapeDtypeStruct((b, t, kvd), F32),
        jax.ShapeDtypeStruct((b, t, kvd), F32),
        jax.ShapeDtypeStruct((b, t, 2 * kvd), BF16),
        jax.ShapeDtypeStruct((b, t, kvd), BF16),
        jax.ShapeDtypeStruct((b, t, N_IDX_HEADS * LANES), BF16),
        jax.ShapeDtypeStruct((b, t, LANES), F32),
        jax.ShapeDtypeStruct((b, t, IDX_DIM), F32),
        jax.ShapeDtypeStruct((b, t, LANES), BF16),
        jax.ShapeDtypeStruct((b, CONV_W - 1, CONV_DIM), F32),
    )
    out_specs = (tile(CONV_DIM), tile(ATTN_DIM), tile(kvd), tile(kvd), tile(2 * kvd), tile(kvd),
                 tile(N_IDX_HEADS * LANES), tile(LANES), tile(IDX_DIM), tile(LANES),
                 pl.BlockSpec((1, CONV_W - 1, CONV_DIM), lambda i, j: (i, 0, 0)))
    return pl.pallas_call(
        _inproj_kernel, grid=grid, in_specs=in_specs, out_specs=out_specs, out_shape=out_shape,
        scratch_shapes=[pltpu.VMEM((8, CONV_DIM), F32)],
        compiler_params=pltpu.CompilerParams(dimension_semantics=("arbitrary", "arbitrary"),
                                             vmem_limit_bytes=V7X_VMEM_LIMIT),
        name="inproj",
    )(x, prev, *consts)


def _slabs(n):
    return [slice(j * LANES, (j + 1) * LANES) for j in range(n // LANES)]


def _tree(op, xs):
    while len(xs) > 1:
        xs = [op(xs[i], xs[i + 1]) for i in range(0, len(xs) - 1, 2)] + ([xs[-1]] if len(xs) % 2 else [])
    return xs[0]


def _attn_kernel(q_ref, iqcat_ref, iw_ref, k2_ref, v_ref, ikcat_ref, o_ref,
                 keyst_ref, khi_ref, klo_ref, keys_ref, lga_ref, lgb_ref, lhs_ref, wb_ref, qg_ref, sa_ref, sb_ref,
                 biasa_ref, biasb_ref, mxa_ref, mxb_ref, p_ref, m_ref, l_ref, acc_ref,
                 *, tq, kc, n_keys, q_pos0, topk):
    qi = pl.program_id(1)
    q0 = q_pos0 + qi * tq
    shift = CHUNK.bit_length() - 1

    def key_limit(pos):
        return jnp.minimum(((pos >> shift) + 1) << shift, n_keys)

    lane = lax.broadcasted_iota(I32, (tq, LANES), 1)
    lim = key_limit(q0 + lax.broadcasted_iota(I32, (tq, LANES), 0))
    qlane = lax.broadcasted_iota(I32, (1, LANES), 1)
    lim_t = jnp.where(qlane < tq, key_limit(q0 + qlane), 0)
    n_chunks = (key_limit(q0 + tq - 1) + kc - 1) // kc
    kfl = jnp.minimum(lim_t, topk).astype(F32)

    iqc = iqcat_ref[0]
    for h in range(N_IDX_HEADS):
        lhs_ref[h * tq:(h + 1) * tq, :] = iqc[:, h * LANES:(h + 1) * LANES]
    iw = iw_ref[0]
    for h in range(N_IDX_HEADS):
        col = jnp.sum(jnp.where(lane == IW_LANE + h, iw, 0.0), axis=1, keepdims=True)
        wb_ref[h] = jnp.broadcast_to(col, (tq, LANES))
    pad_rows = jnp.full((LANES - tq, LANES), INT_MIN, I32) if tq < LANES else None

    last_chunk = keys_ref.shape[0] - 1

    def logits(c, lg_ref):
        off = pl.multiple_of(jnp.minimum(c, last_chunk) * kc, kc)
        lg_ref[...] = lax.dot_general(lhs_ref[...], ikcat_ref[0, pl.ds(off, kc), :],
                                      (((1,), (1,)), ((), ())), preferred_element_type=F32)

    def keys_from(c, lg_ref, kmax):
        for j, sl in enumerate(_slabs(kc)):
            sc = None
            for h in range(N_IDX_HEADS):
                term = jnp.maximum(lg_ref[h * tq:(h + 1) * tq, sl], 0.0) * wb_ref[h]
                sc = term if sc is None else sc + term
            sc = sc + 0.0
            bits = lax.bitcast_convert_type(sc, I32)
            key = bits ^ ((bits >> 31) & INT_MAX)
            key = jnp.where(c * kc + j * LANES + lane < lim, key, INT_MIN)
            keys_ref[c, :, sl] = key
            if pad_rows is not None:
                key = jnp.concatenate([key, pad_rows], axis=0)
            kt = key.T
            keyst_ref[c, sl, :] = kt
            khi_ref[c, sl, :] = (kt >> 16).astype(I16)
            kmax = jnp.maximum(kmax, _tree(jnp.maximum, [kt[8 * i:8 * i + 8, :]
                                                         for i in range(LANES // 8)]))
        return kmax

    logits(0, lga_ref)

    def score_pair(i, kmax):
        logits(2 * i + 1, lgb_ref)
        kmax = keys_from(2 * i, lga_ref, kmax)
        logits(2 * i + 2, lga_ref)
        return keys_from(2 * i + 1, lgb_ref, kmax)

    kmax = lax.fori_loop(0, (n_chunks + 1) // 2, score_pair, jnp.full((8, LANES), INT_MIN, I32))
    kmax = jnp.max(kmax, axis=0, keepdims=True)

    tree_sum = functools.partial(_tree, jnp.add)

    def count_where(pred):
        sub = lax.broadcasted_iota(I32, (kc, LANES), 0)

        def body(c, acc):
            hit = jnp.where(pred(keyst_ref[c], c * kc + sub), 1.0, 0.0)
            return acc + tree_sum([hit[8 * i:8 * i + 8, :] for i in range(kc // 8)])

        part = lax.fori_loop(0, n_chunks, body, jnp.zeros((8, LANES), F32))
        return jnp.sum(part, axis=0, keepdims=True)

    def count16_ge(half_ref, thr):
        thr = thr.astype(I16)

        def body(i, acc):
            hits = []
            for c in (2 * i, 2 * i + 1):
                hit = jnp.where(half_ref[c] >= thr, jnp.int16(1), jnp.int16(0))
                hits += [hit[16 * j:16 * j + 16, :] for j in range(kc // 16)]
            return acc + tree_sum(hits)

        part = lax.fori_loop(0, (n_chunks + 1) // 2, body, jnp.zeros((16, LANES), I16))
        return jnp.sum(part.astype(F32), axis=0, keepdims=True)

    def midpoint(lo, hi):
        return (lo & hi) + ((lo ^ hi) >> 1)

    def search(count_ge, lo, hi, clo, chi, first=None):
        def one_pass(lo, hi, clo, chi, mid=None):
            mid = midpoint(lo, hi) if mid is None else mid
            cnt = count_ge(mid)
            ge = cnt >= kfl
            exact = cnt == kfl
            new = (jnp.where(ge, mid, lo), jnp.where(exact, mid + 1, jnp.where(ge, hi, mid)),
                   jnp.where(ge, cnt, clo), jnp.where(ge, chi, cnt))
            still_open = hi != lo + 1
            return tuple(jnp.where(still_open, n, o) for n, o in zip(new, (lo, hi, clo, chi)))

        def cond(carry):
            return jnp.logical_and(carry[1] > 0.0, carry[0] < 12)

        def body(carry):
            it, _, lo, hi, clo, chi = carry
            for _ in range(BISECT_ROUND):
                lo, hi, clo, chi = one_pass(lo, hi, clo, chi)
            pending = jnp.max(jnp.where(hi != lo + 1, 1.0, 0.0), axis=1, keepdims=True)[0, 0]
            return (it + 1, pending, lo, hi, clo, chi)

        state = (lo, hi, clo, chi) if first is None else one_pass(lo, hi, clo, chi, first)
        return lax.while_loop(cond, body, (jnp.int32(0), jnp.float32(1.0)) + state)[2:]

    half_min = INT_MIN >> 16
    top = kmax >> 16
    lo = jnp.full((1, LANES), half_min + 1, I32)
    hi = jnp.maximum(top + 1, lo + 1)
    probe = jnp.where(top - FIRST_PROBE_GAP > lo, top - FIRST_PROBE_GAP, midpoint(lo, hi))
    pfx, _, clo, above = search(lambda t: count16_ge(khi_ref, t), lo, hi,
                                lim_t.astype(F32), jnp.zeros((1, LANES), F32), probe)

    def low_halves(c, carry):
        k = keyst_ref[c]
        low = (k & 0xFFFF) + half_min
        klo_ref[c] = jnp.where((k >> 16) == pfx, low, half_min).astype(I16)
        return carry

    lax.fori_loop(0, 2 * ((n_chunks + 1) // 2), low_halves, 0)
    lo = jnp.full((1, LANES), half_min, I32)
    hi = jnp.where(clo == kfl, lo + 1, -half_min)
    low, _, clo, chi = search(lambda t: above + count16_ge(klo_ref, t), lo, hi, clo, above)
    thr = (pfx << 16) + (low - half_min)

    extra = clo - kfl
    any_tie = jnp.max(extra, axis=1, keepdims=True)[0, 0]

    @pl.when(any_tie > 0.0)
    def _():
        need = kfl - chi

        def pos_body(_, carry):
            plo, phi = carry
            mid = (plo + phi) >> 1
            cnt = count_where(lambda kch, kpos: jnp.logical_and(kch == thr, kpos <= mid))
            ok = cnt >= need
            return jnp.where(ok, plo, mid), jnp.where(ok, mid, phi)

        n_steps = int(n_keys).bit_length() + 1
        _, last = lax.fori_loop(0, n_steps, pos_body,
                                (jnp.full((1, LANES), -1, I32),
                                 jnp.full((1, LANES), keyst_ref.shape[0] * kc - 1, I32)))
        last = jnp.where(extra > 0.0, last, INT_MAX)
        sub = lax.broadcasted_iota(I32, (kc, LANES), 0)

        def demote(c, carry):
            kch = keyst_ref[c]
            drop = jnp.logical_and(kch == thr, c * kc + sub > last)
            kch = jnp.where(drop, kch - 1, kch)
            keyst_ref[c] = kch
            for sl in _slabs(kc):
                keys_ref[c, :, sl] = kch[sl, :].T[:tq, :]
            return carry

        lax.fori_loop(0, n_chunks, demote, 0)

    thr_rows = jnp.broadcast_to(thr, (LANES, LANES)).T[:tq, :]
    qv = q_ref[0].astype(F32)
    heads_per_group = N_HEADS // N_KV
    rows = heads_per_group * tq
    for g in range(N_KV):
        for r in range(heads_per_group):
            hd = g * heads_per_group + r
            slab = qv[:, (hd // 2) * LANES:(hd // 2 + 1) * LANES]
            keep = (lane < HEAD_DIM) if hd % 2 == 0 else (lane >= HEAD_DIM)
            qg_ref[g, r * tq:(r + 1) * tq, :] = jnp.where(keep, slab, 0.0).astype(BF16)
    m_ref[...] = jnp.full(m_ref.shape, NEG, F32)
    l_ref[...] = jnp.zeros(l_ref.shape, F32)
    acc_ref[...] = jnp.zeros(acc_ref.shape, F32)

    def qk_scores(c, s_ref, bias_ref, mx_ref):
        c = jnp.minimum(c, last_chunk)
        off = pl.multiple_of(c * kc, kc)
        for sl in _slabs(kc):
            bias_ref[:, sl] = jnp.where(keys_ref[c, :, sl] >= thr_rows, 0.0, NEG)
        for g in range(N_KV):
            kch = k2_ref[0, pl.ds(off, kc), g * LANES:(g + 1) * LANES]
            s = lax.dot_general(qg_ref[g], kch, (((1,), (1,)), ((), ())),
                                preferred_element_type=F32)
            s_ref[g] = s
            for r in range(heads_per_group):
                rs = slice(r * tq, (r + 1) * tq)
                part = _tree(jnp.maximum, [s[rs, sl] + bias_ref[:, sl] for sl in _slabs(kc)])
                mx_ref[g, rs, :] = jnp.broadcast_to(jnp.max(part, axis=1, keepdims=True), (tq, LANES))

    def softmax_pv(c, s_ref, bias_ref, mx_ref):
        off = pl.multiple_of(c * kc, kc)
        vch = v_ref[0, pl.ds(off, kc), :]
        for g in range(N_KV):
            m_prev = m_ref[g]
            m_new = jnp.maximum(m_prev, mx_ref[g])
            alpha = jnp.exp2(m_prev - m_new)
            for r in range(heads_per_group):
                rs = slice(r * tq, (r + 1) * tq)
                ps = [jnp.exp2(s_ref[g, rs, sl] + bias_ref[:, sl] - m_new[rs]) for sl in _slabs(kc)]
                p_ref[g, rs, :] = jnp.concatenate(ps, axis=1).astype(BF16)
                l_ref[g, rs, :] = alpha[rs] * l_ref[g, rs, :] + tree_sum(ps)
            acc_ref[g] = alpha * acc_ref[g] + jnp.dot(p_ref[g], vch, preferred_element_type=F32)
            m_ref[g] = m_new

    stage_a = (sa_ref, biasa_ref, mxa_ref)
    stage_b = (sb_ref, biasb_ref, mxb_ref)
    qk_scores(0, *stage_a)

    def attend_pair(i, carry):
        qk_scores(2 * i + 1, *stage_b)
        softmax_pv(2 * i, *stage_a)
        qk_scores(2 * i + 2, *stage_a)
        softmax_pv(2 * i + 1, *stage_b)
        return carry

    lax.fori_loop(0, (n_chunks + 1) // 2, attend_pair, 0)

    for g in range(N_KV):
        out = acc_ref[g] / jnp.sum(l_ref[g], axis=1, keepdims=True)
        for jj in range(heads_per_group // 2):
            a = out[(2 * jj) * tq:(2 * jj + 1) * tq, :]
            bb = out[(2 * jj + 1) * tq:(2 * jj + 2) * tq, :]
            if g % 2 == 0:
                slab = jnp.where(lane < HEAD_DIM, a, pltpu.roll(bb, HEAD_DIM, axis=1))
            else:
                slab = jnp.where(lane < HEAD_DIM, pltpu.roll(a, HEAD_DIM, axis=1), bb)
            so = (g * heads_per_group // 2 + jj) * LANES
            o_ref[0, :, so:so + LANES] = slab.astype(BF16)


def _attention(q, iqcat, iw, k2, vb, ikcat, *, tq, n_keys, q_pos0):
    b, t, _ = q.shape
    lpad = k2.shape[1]
    kc = KEY_CHUNK
    assert lpad % (2 * kc) == 0 and t % tq == 0 and tq <= LANES and CHUNK & (CHUNK - 1) == 0
    topk = min(TOPK_MAX, n_keys // 4)
    kern = functools.partial(_attn_kernel, tq=tq, kc=kc, n_keys=n_keys, q_pos0=q_pos0, topk=topk)
    qtile = lambda n: pl.BlockSpec((1, tq, n), lambda i, j: (i, j, 0))
    ktile = lambda n: pl.BlockSpec((1, lpad, n), lambda i, j: (i, 0, 0))
    rows = (N_HEADS // N_KV) * tq
    return pl.pallas_call(
        kern, grid=(b, t // tq),
        in_specs=[qtile(ATTN_DIM), qtile(N_IDX_HEADS * LANES), qtile(LANES),
                  ktile(2 * N_KV * HEAD_DIM), ktile(N_KV * HEAD_DIM), ktile(LANES)],
        out_specs=qtile(ATTN_DIM),
        out_shape=jax.ShapeDtypeStruct((b, t, ATTN_DIM), BF16),
        scratch_shapes=[
            pltpu.VMEM((lpad // kc, kc, LANES), I32),
            pltpu.VMEM((lpad // kc, kc, LANES), I16),
            pltpu.VMEM((lpad // kc, kc, LANES), I16),
            pltpu.VMEM((lpad // kc, tq, kc), I32),
            pltpu.VMEM((N_IDX_HEADS * tq, kc), F32),
            pltpu.VMEM((N_IDX_HEADS * tq, kc), F32),
            pltpu.VMEM((N_IDX_HEADS * tq, LANES), BF16),
            pltpu.VMEM((N_IDX_HEADS, tq, LANES), F32),
            pltpu.VMEM((N_KV, rows, LANES), BF16),
            pltpu.VMEM((N_KV, rows, kc), F32),
            pltpu.VMEM((N_KV, rows, kc), F32),
            pltpu.VMEM((tq, kc), F32),
            pltpu.VMEM((tq, kc), F32),
            pltpu.VMEM((N_KV, rows, LANES), F32),
            pltpu.VMEM((N_KV, rows, LANES), F32),
            pltpu.VMEM((N_KV, rows, kc), BF16),
            pltpu.VMEM((N_KV, rows, LANES), F32),
            pltpu.VMEM((N_KV, rows, LANES), F32),
            pltpu.VMEM((N_KV, rows, LANES), F32),
        ],
        compiler_params=pltpu.CompilerParams(dimension_semantics=("arbitrary", "arbitrary"),
                                             vmem_limit_bytes=V7X_VMEM_LIMIT),
        name="attention",
    )(q, iqcat, iw, k2, vb, ikcat)


def _mix_in(x_ref, yc_ref, at_ref, woa_ref, wob_ref, g_ref):
    x1 = (x_ref[...] + jnp.dot(yc_ref[...], woa_ref[...], preferred_element_type=F32)
          + jnp.dot(at_ref[...], wob_ref[...], preferred_element_type=F32))
    return x1, _rmsnorm(x1, g_ref[...])


def _swiglu(hb, wg, wu, wd):
    a = jnp.dot(hb, wg, preferred_element_type=F32)
    b = jnp.dot(hb, wu, preferred_element_type=F32)
    act = (a * (1.0 / (1.0 + jnp.exp(-a))) * b).astype(BF16)
    return jnp.dot(act, wd, preferred_element_type=F32)


def _ffn_dense_kernel(x_ref, yc_ref, at_ref, woa_ref, wob_ref, g_ref, wg_ref, wu_ref, wd_ref, o_ref):
    x1, hn = _mix_in(x_ref, yc_ref, at_ref, woa_ref, wob_ref, g_ref)
    o_ref[...] = x1 + _swiglu(hn.astype(BF16), wg_ref[...], wu_ref[...], wd_ref[...])


def _ffn_dense(x, yc, at, lw, tm):
    m, d = x.shape
    row = lambda n: pl.BlockSpec((tm, n), lambda i: (i, 0))
    consts = [lw["wo_a"], lw["wo_b"], lw["ffn_g"], lw["wg"], lw["wu"], lw["wd"]]
    return pl.pallas_call(
        _ffn_dense_kernel, grid=(m // tm,),
        in_specs=[row(d), row(CONV_DIM), row(ATTN_DIM)]
        + [pl.BlockSpec(c.shape, lambda i: (0, 0), pipeline_mode=pl.Buffered(1)) for c in consts],
        out_specs=row(d), out_shape=jax.ShapeDtypeStruct((m, d), F32),
        compiler_params=pltpu.CompilerParams(dimension_semantics=("arbitrary",),
                                             vmem_limit_bytes=V7X_VMEM_LIMIT),
        name="ffn_dense",
    )(x, yc, at, *consts)


def _ffn_moe_kernel(x_ref, yc_ref, at_ref, woa_ref, wob_ref, g_ref, rhi_ref, rlo_ref,
                    eg_ref, eu_ref, ed_ref, o_ref, h_ref, gate_ref, acc_ref):
    e = pl.program_id(1)
    tm = x_ref.shape[0]
    lane = lax.broadcasted_iota(I32, (tm, LANES), 1)

    @pl.when(e == 0)
    def _():
        x1, hn = _mix_in(x_ref, yc_ref, at_ref, woa_ref, wob_ref, g_ref)
        hi, lo = _split_hi_lo(hn)
        h_ref[...] = hi
        rhi = rhi_ref[...]
        logits = (jnp.dot(hi, rhi, preferred_element_type=F32)
                  + jnp.dot(lo, rhi, preferred_element_type=F32)
                  + jnp.dot(hi, rlo_ref[...], preferred_element_type=F32))
        logits = jnp.where(lane < N_EXPERTS, logits, -jnp.inf)
        ex = jnp.exp(logits - jnp.max(logits, axis=1, keepdims=True))
        probs = ex / jnp.sum(ex, axis=1, keepdims=True)
        p1 = jnp.max(probs, axis=1, keepdims=True)
        i1 = jnp.min(jnp.where(probs == p1, lane, LANES), axis=1, keepdims=True)
        rest = jnp.where(lane == i1, -1.0, probs)
        p2 = jnp.max(rest, axis=1, keepdims=True)
        i2 = jnp.min(jnp.where(rest == p2, lane, LANES), axis=1, keepdims=True)
        den = p1 + p2
        gate_ref[...] = jnp.where(lane == i1, p1 / den, jnp.where(lane == i2, p2 / den, 0.0))
        acc_ref[...] = x1

    ge = jnp.sum(jnp.where(lane == e, gate_ref[...], 0.0), axis=1, keepdims=True)
    acc_ref[...] += ge * _swiglu(h_ref[...], eg_ref[0], eu_ref[0], ed_ref[0])

    @pl.when(e == pl.num_programs(1) - 1)
    def _():
        o_ref[...] = acc_ref[...]


def _ffn_moe(x, yc, at, lw, tm):
    m, d = x.shape
    ne, _, fe = lw["eg"].shape
    row = lambda n: pl.BlockSpec((tm, n), lambda i, e: (i, 0))
    consts = [lw["wo_a"], lw["wo_b"], lw["ffn_g"], lw["r_hi"], lw["r_lo"]]
    return pl.pallas_call(
        _ffn_moe_kernel, grid=(m // tm, ne),
        in_specs=[row(d), row(CONV_DIM), row(ATTN_DIM)]
        + [pl.BlockSpec(c.shape, lambda i, e: (0, 0), pipeline_mode=pl.Buffered(1)) for c in consts]
        + [pl.BlockSpec((1, d, fe), lambda i, e: (e, 0, 0)),
           pl.BlockSpec((1, d, fe), lambda i, e: (e, 0, 0)),
           pl.BlockSpec((1, fe, d), lambda i, e: (e, 0, 0))],
        out_specs=row(d), out_shape=jax.ShapeDtypeStruct((m, d), F32),
        scratch_shapes=[pltpu.VMEM((tm, d), BF16), pltpu.VMEM((tm, LANES), F32),
                        pltpu.VMEM((tm, d), F32)],
        compiler_params=pltpu.CompilerParams(dimension_semantics=("arbitrary", "arbitrary"),
                                             vmem_limit_bytes=V7X_VMEM_LIMIT),
        name="ffn_moe",
    )(x, yc, at, *consts, lw["eg"], lw["eu"], lw["ed"])


def _block_diag_mean(n, group, valid=None):
    m = np.zeros((n, n), np.float32)
    for s in range(0, n if valid is None else valid, group):
        m[s:s + group, s:s + group] = 1.0 / group
    return jnp.asarray(m, BF16)


def _placement_matrices():
    s1 = np.zeros((N_IDX_HEADS * IDX_DIM, N_IDX_HEADS * LANES), np.float32)
    s2 = np.zeros_like(s1)
    t1 = np.zeros((LANES, LANES), np.float32)
    t2 = np.zeros_like(t1)
    j = np.arange(IDX_DIM)
    for h in range(N_IDX_HEADS):
        s1[h * IDX_DIM + j, h * LANES + j] = 1.0
        s2[h * IDX_DIM + j, h * LANES + IDX_DIM + j] = 1.0
        s1[h * IDX_DIM + j, h * LANES + 2 * IDX_DIM + j] = 1.0
    t1[j, j] = 1.0
    t1[j, IDX_DIM + j] = 1.0
    t2[j, 2 * IDX_DIM + j] = 1.0
    return tuple(jnp.asarray(a, BF16) for a in (s1, s2, t1, t2))


def _layer_weights(l, p):
    d = p["w_in"].shape[1]
    s1, s2, t1, t2 = _placement_matrices()
    lw = {
        "attn_g": p["attn_norm_g"][l][None],
        "w_in": jnp.pad(p["w_in"][l], ((0, 0), (0, IN_COLS_PAD - IN_COLS))).astype(BF16),
        "conv_w": p["conv_w"][l],
        "gq": jnp.tile(p["q_norm_g"][l], N_HEADS)[None],
        "gk": jnp.tile(p["k_norm_g"][l], N_KV)[None],
        "giq": jnp.tile(p["iq_norm_g"][l], N_IDX_HEADS)[None],
        "gik": jnp.pad(p["ik_norm_g"][l], (0, LANES - IDX_DIM))[None],
        "mq": _block_diag_mean(ATTN_DIM, HEAD_DIM),
        "mk": _block_diag_mean(N_KV * HEAD_DIM, HEAD_DIM),
        "miq": _block_diag_mean(N_IDX_HEADS * IDX_DIM, IDX_DIM),
        "mik": _block_diag_mean(LANES, IDX_DIM, valid=IDX_DIM),
        "s1": s1, "s2": s2, "t1": t1, "t2": t2,
        "wo_a": p["w_out"][l][:CONV_DIM].astype(BF16),
        "wo_b": p["w_out"][l][CONV_DIM:].astype(BF16),
        "ffn_g": p["ffn_norm_g"][l][None],
    }
    j = l // 2
    if l % 2 == 0:
        lw.update(wg=p["ffn_w_gate"][j].astype(BF16), wu=p["ffn_w_up"][j].astype(BF16),
                  wd=p["ffn_w_down"][j].astype(BF16))
    else:
        r = jnp.pad(p["router_w"][j], ((0, 0), (0, LANES - N_EXPERTS)))
        r_hi = r.astype(BF16)
        lw.update(r_hi=r_hi, r_lo=(r - r_hi.astype(F32)).astype(BF16),
                  eg=p["moe_w_gate"][j].astype(BF16), eu=p["moe_w_up"][j].astype(BF16),
                  ed=p["moe_w_down"][j].astype(BF16))
    assert d == lw["wo_a"].shape[1]
    return lw


def _pick_tile(n, pref):
    t = min(n, pref)
    while n % t:
        t //= 2
    return t


def _mixer(x, prev, lw, *, tq, past_k2=None, past_v=None, past_ik=None, q_pos0=0):
    b, t, d = x.shape
    yc, q, k, v, k2, vb, iqcat, iw, ik, ikcat, cst = _inproj(x, prev, lw, _pick_tile(t, 512))
    if past_k2 is not None:
        k2 = jnp.concatenate([past_k2, k2], axis=1)
        vb = jnp.concatenate([past_v, vb], axis=1)
        ikcat = jnp.concatenate([past_ik, ikcat], axis=1)
    n_keys = k2.shape[1]
    pad = (-n_keys) % (2 * KEY_CHUNK)
    if pad:
        k2, vb, ikcat = (jnp.pad(a, ((0, 0), (0, pad), (0, 0))) for a in (k2, vb, ikcat))
    at = _attention(q, iqcat, iw, k2, vb, ikcat, tq=tq, n_keys=n_keys, q_pos0=q_pos0)
    m = b * t
    x2, yc2, at2 = x.reshape(m, d), yc.reshape(m, CONV_DIM), at.reshape(m, ATTN_DIM)
    if "wg" in lw:
        y = _ffn_dense(x2, yc2, at2, lw, _pick_tile(m, 512))
    else:
        y = _ffn_moe(x2, yc2, at2, lw, _pick_tile(m, 512))
    return (y.reshape(b, t, d), k.reshape(b, t, N_KV, HEAD_DIM), v.reshape(b, t, N_KV, HEAD_DIM),
            ik, cst)


def kernel(x_prompt, x_sample, cache_k, cache_v, cache_ik, state_conv, attn_norm_g, w_in, conv_w,
           q_norm_g, k_norm_g, iq_norm_g, ik_norm_g, w_out, ffn_norm_g, ffn_w_gate, ffn_w_up,
           ffn_w_down, router_w, moe_w_gate, moe_w_up, moe_w_down):
    params = dict(attn_norm_g=attn_norm_g, w_in=w_in, conv_w=conv_w, q_norm_g=q_norm_g,
                  k_norm_g=k_norm_g, iq_norm_g=iq_norm_g, ik_norm_g=ik_norm_g, w_out=w_out,
                  ffn_norm_g=ffn_norm_g, ffn_w_gate=ffn_w_gate, ffn_w_up=ffn_w_up,
                  ffn_w_down=ffn_w_down, router_w=router_w, moe_w_gate=moe_w_gate,
                  moe_w_up=moe_w_up, moe_w_down=moe_w_down)
    depth = w_in.shape[0]
    xp, xs = x_prompt, x_sample
    bp = xp.shape[0]
    bs, ts, _ = xs.shape
    past = cache_k.shape[2]
    kvd = N_KV * HEAD_DIM
    outs = [[] for _ in range(8)]
    for l in range(depth):
        lw = _layer_weights(l, params)
        prev0 = jnp.zeros((bp, CONV_W - 1, CONV_DIM), F32)
        xp, kp, vp, ikp, cp = _mixer(xp, prev0, lw, tq=Q_BLOCK)

        ck = cache_k[l].reshape(bs, past, kvd)
        g0, g1 = ck[..., :HEAD_DIM], ck[..., HEAD_DIM:]
        past_k2 = jnp.concatenate([g0, g0, g1, g1], axis=-1).astype(BF16)
        past_v = cache_v[l].reshape(bs, past, kvd).astype(BF16)
        cik = cache_ik[l]
        cik_hi = cik.astype(BF16)
        cik_lo = (cik - cik_hi.astype(F32)).astype(BF16)
        past_ik = jnp.concatenate([cik_hi, cik_hi, cik_lo, jnp.zeros_like(cik_hi)], axis=-1)
        xs, ks, vs, iks, cs = _mixer(xs, state_conv[l], lw, tq=ts, past_k2=past_k2,
                                     past_v=past_v, past_ik=past_ik, q_pos0=past)
        for lst, val in zip(outs, (kp, vp, ikp, cp, ks, vs, iks, cs)):
            lst.append(val)
    kp, vp, ikp, cp, ks, vs, iks, cs = (jnp.stack(o) for o in outs)
    return (xp, xs, kp, vp, ikp, cp, ks, vs, iks, cs)
```

```python
import functools
import math

import numpy as np
import jax
import jax.numpy as jnp
from jax import lax
from jax.experimental import pallas as pl
from jax.experimental.pallas import tpu as pltpu

CHUNK = 64
CONV_DIM = 512
CONV_W = 3
N_HEADS = 8
HEAD_DIM = 64
ATTN_DIM = N_HEADS * HEAD_DIM
N_KV = 2
N_IDX_HEADS = 8
IDX_DIM = 32
TOPK_MAX = 256
Q_BLOCK = 128
N_EXPERTS = 8
EPS = 1e-6
COL_SIZES = (CONV_DIM, CONV_DIM, CONV_DIM, ATTN_DIM, N_KV * HEAD_DIM, N_KV * HEAD_DIM,
             N_IDX_HEADS * IDX_DIM, IDX_DIM, N_IDX_HEADS)
IN_COLS = sum(COL_SIZES)

LANES = 128
V7X_VMEM_LIMIT = 56 * 1024 * 1024

IN_COLS_PAD = ((IN_COLS + LANES - 1) // LANES) * LANES
TAIL_OFF = IN_COLS_PAD - LANES
IW_LANE = IDX_DIM
QSCALE = HEAD_DIM ** -0.5 * math.log2(math.e)
IWSCALE = IDX_DIM ** -0.5 * N_IDX_HEADS ** -0.5
NEG = -1e30
INT_MIN = -2 ** 31
INT_MAX = 2 ** 31 - 1
KEY_CHUNK = 512
BISECT_ROUND = 4
BISECT_MAX_ROUNDS = -(-(32 + 3) // BISECT_ROUND)
FIRST_PROBE_GAP = 1 << 24

F32 = jnp.float32
BF16 = jnp.bfloat16
I32 = jnp.int32


def _split_hi_lo(x):
    hi = x.astype(BF16)
    lo = (x - hi.astype(F32)).astype(BF16)
    return hi, lo


def _group_rsqrt(x, g_ref):
    hi, lo = _split_hi_lo(x * x)
    g = g_ref[...]
    ms = jnp.dot(hi, g, preferred_element_type=F32) + jnp.dot(lo, g, preferred_element_type=F32)
    return lax.rsqrt(ms + EPS)


def _rmsnorm(x, g):
    return x * lax.rsqrt(jnp.mean(x * x, axis=-1, keepdims=True) + EPS) * g


def _inproj_kernel(x_ref, prev_ref, g_ref, w_ref, cw_ref, gq_ref, gk_ref, giq_ref, gik_ref,
                   mq_ref, mk_ref, miq_ref, mik_ref, s1_ref, s2_ref, t1_ref, t2_ref,
                   yconv_ref, q_ref, k_ref, v_ref, k2_ref, vb_ref, iqcat_ref, iw_ref, ik_ref,
                   ikcat_ref, cst_ref, carry_ref):
    t = pl.program_id(1)
    nt = pl.num_programs(1)
    x = x_ref[0]
    tm = x.shape[0]
    h = _rmsnorm(x, g_ref[...]).astype(BF16)
    p = jnp.dot(h, w_ref[...], preferred_element_type=F32)
    o = np.cumsum((0,) + COL_SIZES)
    c_gate, b_gate, u = p[:, o[0]:o[1]], p[:, o[1]:o[2]], p[:, o[2]:o[3]]
    q, k, v, iq = p[:, o[3]:o[4]], p[:, o[4]:o[5]], p[:, o[5]:o[6]], p[:, o[6]:o[7]]
    tail = p[:, TAIL_OFF:]

    ci = c_gate * u

    @pl.when(t == 0)
    def _():
        carry_ref[8 - (CONV_W - 1):, :] = prev_ref[0]

    row = lax.broadcasted_iota(I32, ci.shape, 0)
    c7 = carry_ref[7:8, :]
    c6 = carry_ref[6:7, :]
    s1 = jnp.where(row == 0, c7, pltpu.roll(ci, 1, axis=0))
    s2 = jnp.where(row == 0, c6, jnp.where(row == 1, c7, pltpu.roll(ci, 2, axis=0)))
    cw = cw_ref[...]
    conv = cw[0:1] * s2 + cw[1:2] * s1 + cw[2:3] * ci
    yconv_ref[0] = (b_gate * conv).astype(BF16)
    carry_ref[...] = ci[tm - 8:, :]

    @pl.when(t == nt - 1)
    def _():
        cst_ref[0] = ci[tm - (CONV_W - 1):, :]

    qn = q * _group_rsqrt(q, mq_ref) * gq_ref[...]
    q_ref[0] = (qn * QSCALE).astype(BF16)
    kn = k * _group_rsqrt(k, mk_ref) * gk_ref[...]
    k_ref[0] = kn
    v_ref[0] = v
    lane = lax.broadcasted_iota(I32, kn.shape, 1)
    kr = pltpu.roll(kn, HEAD_DIM, axis=1)
    k2_ref[0, :, 0:LANES] = jnp.where(lane < HEAD_DIM, kn, kr).astype(BF16)
    k2_ref[0, :, LANES:2 * LANES] = jnp.where(lane < HEAD_DIM, kr, kn).astype(BF16)
    vb_ref[0] = v.astype(BF16)

    iqn = iq * _group_rsqrt(iq, miq_ref) * giq_ref[...]
    ihi, ilo = _split_hi_lo(iqn)
    iqcat_ref[0] = (jnp.dot(ihi, s1_ref[...], preferred_element_type=F32)
                    + jnp.dot(ilo, s2_ref[...], preferred_element_type=F32)).astype(BF16)
    tn = tail * _group_rsqrt(tail, mik_ref) * gik_ref[...]
    ik_ref[0] = tn[:, 0:IDX_DIM]
    thi, tlo = _split_hi_lo(tn)
    ikcat_ref[0] = (jnp.dot(thi, t1_ref[...], preferred_element_type=F32)
                    + jnp.dot(tlo, t2_ref[...], preferred_element_type=F32)).astype(BF16)
    iw_ref[0] = tail * IWSCALE


def _const_spec(shape):
    nd = len(shape)
    return pl.BlockSpec(shape, lambda *_: (0,) * nd)


def _inproj(x, prev, lw, tm):
    b, t, d = x.shape
    grid = (b, t // tm)
    tile = lambda n: pl.BlockSpec((1, tm, n), lambda i, j: (i, j, 0))
    consts = [lw["attn_g"], lw["w_in"], lw["conv_w"], lw["gq"], lw["gk"], lw["giq"], lw["gik"],
              lw["mq"], lw["mk"], lw["miq"], lw["mik"], lw["s1"], lw["s2"], lw["t1"], lw["t2"]]
    in_specs = ([tile(d), pl.BlockSpec((1, CONV_W - 1, CONV_DIM), lambda i, j: (i, 0, 0))]
                + [_const_spec(c.shape) for c in consts])
    kvd = N_KV * HEAD_DIM
    out_shape = (
        jax.ShapeDtypeStruct((b, t, CONV_DIM), BF16),
        jax.ShapeDtypeStruct((b, t, ATTN_DIM), BF16),
        jax.ShapeDtypeStruct((b, t, kvd), F32),
        jax.ShapeDtypeStruct((b, t, kvd), F32),
        jax.ShapeDtypeStruct((b, t, 2 * kvd), BF16),
        jax.ShapeDtypeStruct((b, t, kvd), BF16),
        jax.ShapeDtypeStruct((b, t, N_IDX_HEADS * LANES), BF16),
        jax.ShapeDtypeStruct((b, t, LANES), F32),
        jax.ShapeDtypeStruct((b, t, IDX_DIM), F32),
        jax.ShapeDtypeStruct((b, t, LANES), BF16),
        jax.ShapeDtypeStruct((b, CONV_W - 1, CONV_DIM), F32),
    )
    out_specs = (tile(CONV_DIM), tile(ATTN_DIM), tile(kvd), tile(kvd), tile(2 * kvd), tile(kvd),
                 tile(N_IDX_HEADS * LANES), tile(LANES), tile(IDX_DIM), tile(LANES),
                 pl.BlockSpec((1, CONV_W - 1, CONV_DIM), lambda i, j: (i, 0, 0)))
    return pl.pallas_call(
        _inproj_kernel, grid=grid, in_specs=in_specs, out_specs=out_specs, out_shape=out_shape,
        scratch_shapes=[pltpu.VMEM((8, CONV_DIM), F32)],
        compiler_params=pltpu.CompilerParams(dimension_semantics=("arbitrary", "arbitrary"),
                                             vmem_limit_bytes=V7X_VMEM_LIMIT),
        name="inproj",
    )(x, prev, *consts)


def _slabs(n):
    return [slice(j * LANES, (j + 1) * LANES) for j in range(n // LANES)]


def _tree(op, xs):
    while len(xs) > 1:
        xs = [op(xs[i], xs[i + 1]) for i in range(0, len(xs) - 1, 2)] + ([xs[-1]] if len(xs) % 2 else [])
    return xs[0]


def _attn_kernel(q_ref, iqcat_ref, iw_ref, k2_ref, v_ref, ikcat_ref, tri_ref, o_ref,
                 keyst_ref, keys_ref, lga_ref, lgb_ref, lhs_ref, wb_ref, qg_ref, sa_ref, sb_ref,
                 biasa_ref, biasb_ref, mxa_ref, mxb_ref, p_ref, m_ref, l_ref, acc_ref,
                 *, tq, kc, n_keys, q_pos0, topk):
    qi = pl.program_id(1)
    q0 = q_pos0 + qi * tq
    shift = CHUNK.bit_length() - 1

    def key_limit(pos):
        return jnp.minimum(((pos >> shift) + 1) << shift, n_keys)

    lane = lax.broadcasted_iota(I32, (tq, LANES), 1)
    lim = key_limit(q0 + lax.broadcasted_iota(I32, (tq, LANES), 0))
    qlane = lax.broadcasted_iota(I32, (1, LANES), 1)
    lim_t = jnp.where(qlane < tq, key_limit(q0 + qlane), 0)
    n_chunks = (key_limit(q0 + tq - 1) + kc - 1) // kc
    kfl = jnp.minimum(lim_t, topk).astype(F32)

    iqc = iqcat_ref[0]
    for h in range(N_IDX_HEADS):
        lhs_ref[h * tq:(h + 1) * tq, :] = iqc[:, h * LANES:(h + 1) * LANES]
    iw = iw_ref[0]
    for h in range(N_IDX_HEADS):
        col = jnp.sum(jnp.where(lane == IW_LANE + h, iw, 0.0), axis=1, keepdims=True)
        wb_ref[h] = jnp.broadcast_to(col, (tq, LANES))
    pad_rows = jnp.full((LANES - tq, LANES), INT_MIN, I32) if tq < LANES else None

    last_chunk = keys_ref.shape[0] - 1

    def logits(c, lg_ref):
        off = pl.multiple_of(jnp.minimum(c, last_chunk) * kc, kc)
        lg_ref[...] = lax.dot_general(lhs_ref[...], ikcat_ref[0, pl.ds(off, kc), :],
                                      (((1,), (1,)), ((), ())), preferred_element_type=F32)

    def keys_from(c, lg_ref, kmax):
        for j, sl in enumerate(_slabs(kc)):
            sc = None
            for h in range(N_IDX_HEADS):
                term = jnp.maximum(lg_ref[h * tq:(h + 1) * tq, sl], 0.0) * wb_ref[h]
                sc = term if sc is None else sc + term
            sc = sc + 0.0
            bits = lax.bitcast_convert_type(sc, I32)
            key = bits ^ ((bits >> 31) & INT_MAX)
            key = jnp.where(c * kc + j * LANES + lane < lim, key, INT_MIN)
            keys_ref[c, :, sl] = key
            if pad_rows is not None:
                key = jnp.concatenate([key, pad_rows], axis=0)
            kt = key.T
            keyst_ref[c, sl, :] = kt
            kmax = jnp.maximum(kmax, _tree(jnp.maximum, [kt[8 * i:8 * i + 8, :]
                                                         for i in range(LANES // 8)]))
        return kmax

    logits(0, lga_ref)

    def score_pair(i, kmax):
        logits(2 * i + 1, lgb_ref)
        kmax = keys_from(2 * i, lga_ref, kmax)
        logits(2 * i + 2, lga_ref)
        return keys_from(2 * i + 1, lgb_ref, kmax)

    kmax = lax.fori_loop(0, (n_chunks + 1) // 2, score_pair, jnp.full((8, LANES), INT_MIN, I32))
    kmax = jnp.max(kmax, axis=0, keepdims=True)

    tree_sum = functools.partial(_tree, jnp.add)

    def count_ge(thr):
        def body(c, acc):
            hit = jnp.where(keyst_ref[c] >= thr, 1.0, 0.0)
            return acc + tree_sum([hit[8 * i:8 * i + 8, :] for i in range(kc // 8)])

        part = lax.fori_loop(0, n_chunks, body, jnp.zeros((8, LANES), F32))
        return jnp.sum(part, axis=0, keepdims=True)

    lo0 = jnp.full((1, LANES), INT_MIN + 1, I32)

    def next_probe(lo, hi):
        mid = (lo & hi) + ((lo ^ hi) >> 1)
        floor = lo == lo0
        return jnp.where(jnp.logical_and(floor, hi > 1), 1,
                         jnp.where(jnp.logical_and(floor, hi == 1), 0, mid))

    def bisect_pass(lo, hi, clo, chi, mid):
        cnt = count_ge(mid)
        ge = cnt >= kfl
        exact = cnt == kfl
        lo_n = jnp.where(ge, mid, lo)
        hi_n = jnp.where(exact, mid + 1, jnp.where(ge, hi, mid))
        return lo_n, hi_n, jnp.where(ge, cnt, clo), jnp.where(ge, chi, cnt)

    def bisect_cond(carry):
        it, pending = carry[0], carry[1]
        return jnp.logical_and(pending > 0.0, it < BISECT_MAX_ROUNDS)

    def bisect_body(carry):
        it, _, lo, hi, clo, chi = carry
        for _ in range(BISECT_ROUND):
            lo, hi, clo, chi = bisect_pass(lo, hi, clo, chi, next_probe(lo, hi))
        pending = jnp.max(jnp.where(hi != lo + 1, 1.0, 0.0), axis=1, keepdims=True)[0, 0]
        return (it + 1, pending, lo, hi, clo, chi)

    hi = jnp.maximum(jnp.minimum(kmax, INT_MAX - 1) + 1, lo0 + 1)
    first = jnp.where(kmax > INT_MIN + 2 + FIRST_PROBE_GAP, kmax - FIRST_PROBE_GAP, next_probe(lo0, hi))
    state = bisect_pass(lo0, hi, lim_t.astype(F32), jnp.zeros((1, LANES), F32), first)
    _, _, thr, _, clo, chi = lax.while_loop(bisect_cond, bisect_body,
                                            (jnp.int32(0), jnp.float32(1.0)) + state)

    extra = clo - kfl
    any_tie = jnp.max(extra, axis=1, keepdims=True)[0, 0]

    @pl.when(any_tie > 0.0)
    def _():
        need = jnp.where(extra > 0.0, kfl - chi, jnp.inf)

        def demote(c, seen):
            kch = keyst_ref[c]
            tied = kch == thr
            rank = seen + jnp.dot(tri_ref[...], jnp.where(tied, 1.0, 0.0).astype(BF16),
                                  preferred_element_type=F32)
            kch = jnp.where(jnp.logical_and(tied, rank > need), kch - 1, kch)
            keyst_ref[c] = kch
            for sl in _slabs(kc):
                keys_ref[c, :, sl] = kch[sl, :].T[:tq, :]
            return rank[kc - 1:kc, :]

        lax.fori_loop(0, n_chunks, demote, jnp.zeros((1, LANES), F32))

    thr_rows = jnp.broadcast_to(thr, (LANES, LANES)).T[:tq, :]
    qv = q_ref[0].astype(F32)
    heads_per_group = N_HEADS // N_KV
    rows = heads_per_group * tq
    for g in range(N_KV):
        for r in range(heads_per_group):
            hd = g * heads_per_group + r
            slab = qv[:, (hd // 2) * LANES:(hd // 2 + 1) * LANES]
            keep = (lane < HEAD_DIM) if hd % 2 == 0 else (lane >= HEAD_DIM)
            qg_ref[g, r * tq:(r + 1) * tq, :] = jnp.where(keep, slab, 0.0).astype(BF16)
    m_ref[...] = jnp.full(m_ref.shape, NEG, F32)
    l_ref[...] = jnp.zeros(l_ref.shape, F32)
    acc_ref[...] = jnp.zeros(acc_ref.shape, F32)

    def qk_scores(c, s_ref, bias_ref, mx_ref):
        c = jnp.minimum(c, last_chunk)
        off = pl.multiple_of(c * kc, kc)
        for sl in _slabs(kc):
            bias_ref[:, sl] = jnp.where(keys_ref[c, :, sl] >= thr_rows, 0.0, NEG)
        for g in range(N_KV):
            kch = k2_ref[0, pl.ds(off, kc), g * LANES:(g + 1) * LANES]
            s = lax.dot_general(qg_ref[g], kch, (((1,), (1,)), ((), ())),
                                preferred_element_type=F32)
            s_ref[g] = s
            for r in range(heads_per_group):
                rs = slice(r * tq, (r + 1) * tq)
                part = _tree(jnp.maximum, [s[rs, sl] + bias_ref[:, sl] for sl in _slabs(kc)])
                mx_ref[g, rs, :] = jnp.broadcast_to(jnp.max(part, axis=1, keepdims=True), (tq, LANES))

    def softmax_pv(c, s_ref, bias_ref, mx_ref):
        off = pl.multiple_of(c * kc, kc)
        vch = v_ref[0, pl.ds(off, kc), :]
        for g in range(N_KV):
            m_prev = m_ref[g]
            m_new = jnp.maximum(m_prev, mx_ref[g])
            alpha = jnp.exp2(m_prev - m_new)
            for r in range(heads_per_group):
                rs = slice(r * tq, (r + 1) * tq)
                ps = [jnp.exp2(s_ref[g, rs, sl] + bias_ref[:, sl] - m_new[rs]) for sl in _slabs(kc)]
                p_ref[g, rs, :] = jnp.concatenate(ps, axis=1).astype(BF16)
                l_ref[g, rs, :] = alpha[rs] * l_ref[g, rs, :] + tree_sum(ps)
            acc_ref[g] = alpha * acc_ref[g] + jnp.dot(p_ref[g], vch, preferred_element_type=F32)
            m_ref[g] = m_new

    stage_a = (sa_ref, biasa_ref, mxa_ref)
    stage_b = (sb_ref, biasb_ref, mxb_ref)
    qk_scores(0, *stage_a)

    def attend_pair(i, carry):
        qk_scores(2 * i + 1, *stage_b)
        softmax_pv(2 * i, *stage_a)
        qk_scores(2 * i + 2, *stage_a)
        softmax_pv(2 * i + 1, *stage_b)
        return carry

    lax.fori_loop(0, (n_chunks + 1) // 2, attend_pair, 0)

    for g in range(N_KV):
        out = acc_ref[g] / jnp.sum(l_ref[g], axis=1, keepdims=True)
        for jj in range(heads_per_group // 2):
            a = out[(2 * jj) * tq:(2 * jj + 1) * tq, :]
            bb = out[(2 * jj + 1) * tq:(2 * jj + 2) * tq, :]
            if g % 2 == 0:
                slab = jnp.where(lane < HEAD_DIM, a, pltpu.roll(bb, HEAD_DIM, axis=1))
            else:
                slab = jnp.where(lane < HEAD_DIM, pltpu.roll(a, HEAD_DIM, axis=1), bb)
            so = (g * heads_per_group // 2 + jj) * LANES
            o_ref[0, :, so:so + LANES] = slab.astype(BF16)


def _attention(q, iqcat, iw, k2, vb, ikcat, *, tq, n_keys, q_pos0):
    b, t, _ = q.shape
    lpad = k2.shape[1]
    kc = KEY_CHUNK
    assert lpad % (2 * kc) == 0 and t % tq == 0 and tq <= LANES and CHUNK & (CHUNK - 1) == 0
    topk = min(TOPK_MAX, n_keys // 4)
    kern = functools.partial(_attn_kernel, tq=tq, kc=kc, n_keys=n_keys, q_pos0=q_pos0, topk=topk)
    qtile = lambda n: pl.BlockSpec((1, tq, n), lambda i, j: (i, j, 0))
    ktile = lambda n: pl.BlockSpec((1, lpad, n), lambda i, j: (i, 0, 0))
    rows = (N_HEADS // N_KV) * tq
    return pl.pallas_call(
        kern, grid=(b, t // tq),
        in_specs=[qtile(ATTN_DIM), qtile(N_IDX_HEADS * LANES), qtile(LANES),
                  ktile(2 * N_KV * HEAD_DIM), ktile(N_KV * HEAD_DIM), ktile(LANES),
                  pl.BlockSpec((kc, kc), lambda i, j: (0, 0))],
        out_specs=qtile(ATTN_DIM),
        out_shape=jax.ShapeDtypeStruct((b, t, ATTN_DIM), BF16),
        scratch_shapes=[
            pltpu.VMEM((lpad // kc, kc, LANES), I32),
            pltpu.VMEM((lpad // kc, tq, kc), I32),
            pltpu.VMEM((N_IDX_HEADS * tq, kc), F32),
            pltpu.VMEM((N_IDX_HEADS * tq, kc), F32),
            pltpu.VMEM((N_IDX_HEADS * tq, LANES), BF16),
            pltpu.VMEM((N_IDX_HEADS, tq, LANES), F32),
            pltpu.VMEM((N_KV, rows, LANES), BF16),
            pltpu.VMEM((N_KV, rows, kc), F32),
            pltpu.VMEM((N_KV, rows, kc), F32),
            pltpu.VMEM((tq, kc), F32),
            pltpu.VMEM((tq, kc), F32),
            pltpu.VMEM((N_KV, rows, LANES), F32),
            pltpu.VMEM((N_KV, rows, LANES), F32),
            pltpu.VMEM((N_KV, rows, kc), BF16),
            pltpu.VMEM((N_KV, rows, LANES), F32),
            pltpu.VMEM((N_KV, rows, LANES), F32),
            pltpu.VMEM((N_KV, rows, LANES), F32),
        ],
        compiler_params=pltpu.CompilerParams(dimension_semantics=("arbitrary", "arbitrary"),
                                             vmem_limit_bytes=V7X_VMEM_LIMIT),
        name="attention",
    )(q, iqcat, iw, k2, vb, ikcat, jnp.asarray(np.tril(np.ones((kc, kc), np.float32)), BF16))


def _mix_in(x_ref, yc_ref, at_ref, woa_ref, wob_ref, g_ref):
    x1 = (x_ref[...] + jnp.dot(yc_ref[...], woa_ref[...], preferred_element_type=F32)
          + jnp.dot(at_ref[...], wob_ref[...], preferred_element_type=F32))
    return x1, _rmsnorm(x1, g_ref[...])


def _swiglu(hb, wg, wu, wd):
    a = jnp.dot(hb, wg, preferred_element_type=F32)
    b = jnp.dot(hb, wu, preferred_element_type=F32)
    act = (a * (1.0 / (1.0 + jnp.exp(-a))) * b).astype(BF16)
    return jnp.dot(act, wd, preferred_element_type=F32)


def _ffn_dense_kernel(x_ref, yc_ref, at_ref, woa_ref, wob_ref, g_ref, wg_ref, wu_ref, wd_ref, o_ref):
    x1, hn = _mix_in(x_ref, yc_ref, at_ref, woa_ref, wob_ref, g_ref)
    o_ref[...] = x1 + _swiglu(hn.astype(BF16), wg_ref[...], wu_ref[...], wd_ref[...])


def _ffn_dense(x, yc, at, lw, tm):
    m, d = x.shape
    row = lambda n: pl.BlockSpec((tm, n), lambda i: (i, 0))
    consts = [lw["wo_a"], lw["wo_b"], lw["ffn_g"], lw["wg"], lw["wu"], lw["wd"]]
    return pl.pallas_call(
        _ffn_dense_kernel, grid=(m // tm,),
        in_specs=[row(d), row(CONV_DIM), row(ATTN_DIM)]
        + [pl.BlockSpec(c.shape, lambda i: (0, 0), pipeline_mode=pl.Buffered(1)) for c in consts],
        out_specs=row(d), out_shape=jax.ShapeDtypeStruct((m, d), F32),
        compiler_params=pltpu.CompilerParams(dimension_semantics=("arbitrary",),
                                             vmem_limit_bytes=V7X_VMEM_LIMIT),
        name="ffn_dense",
    )(x, yc, at, *consts)


def _ffn_moe_kernel(x_ref, yc_ref, at_ref, woa_ref, wob_ref, g_ref, rhi_ref, rlo_ref,
                    eg_ref, eu_ref, ed_ref, o_ref, h_ref, gate_ref, acc_ref):
    e = pl.program_id(1)
    tm = x_ref.shape[0]
    lane = lax.broadcasted_iota(I32, (tm, LANES), 1)

    @pl.when(e == 0)
    def _():
        x1, hn = _mix_in(x_ref, yc_ref, at_ref, woa_ref, wob_ref, g_ref)
        hi, lo = _split_hi_lo(hn)
        h_ref[...] = hi
        rhi = rhi_ref[...]
        logits = (jnp.dot(hi, rhi, preferred_element_type=F32)
                  + jnp.dot(lo, rhi, preferred_element_type=F32)
                  + jnp.dot(hi, rlo_ref[...], preferred_element_type=F32))
        logits = jnp.where(lane < N_EXPERTS, logits, -jnp.inf)
        ex = jnp.exp(logits - jnp.max(logits, axis=1, keepdims=True))
        probs = ex / jnp.sum(ex, axis=1, keepdims=True)
        p1 = jnp.max(probs, axis=1, keepdims=True)
        i1 = jnp.min(jnp.where(probs == p1, lane, LANES), axis=1, keepdims=True)
        rest = jnp.where(lane == i1, -1.0, probs)
        p2 = jnp.max(rest, axis=1, keepdims=True)
        i2 = jnp.min(jnp.where(rest == p2, lane, LANES), axis=1, keepdims=True)
        den = p1 + p2
        gate_ref[...] = jnp.where(lane == i1, p1 / den, jnp.where(lane == i2, p2 / den, 0.0))
        acc_ref[...] = x1

    ge = jnp.sum(jnp.where(lane == e, gate_ref[...], 0.0), axis=1, keepdims=True)
    acc_ref[...] += ge * _swiglu(h_ref[...], eg_ref[0], eu_ref[0], ed_ref[0])

    @pl.when(e == pl.num_programs(1) - 1)
    def _():
        o_ref[...] = acc_ref[...]


def _ffn_moe(x, yc, at, lw, tm):
    m, d = x.shape
    ne, _, fe = lw["eg"].shape
    row = lambda n: pl.BlockSpec((tm, n), lambda i, e: (i, 0))
    consts = [lw["wo_a"], lw["wo_b"], lw["ffn_g"], lw["r_hi"], lw["r_lo"]]
    return pl.pallas_call(
        _ffn_moe_kernel, grid=(m // tm, ne),
        in_specs=[row(d), row(CONV_DIM), row(ATTN_DIM)]
        + [pl.BlockSpec(c.shape, lambda i, e: (0, 0), pipeline_mode=pl.Buffered(1)) for c in consts]
        + [pl.BlockSpec((1, d, fe), lambda i, e: (e, 0, 0)),
           pl.BlockSpec((1, d, fe), lambda i, e: (e, 0, 0)),
           pl.BlockSpec((1, fe, d), lambda i, e: (e, 0, 0))],
        out_specs=row(d), out_shape=jax.ShapeDtypeStruct((m, d), F32),
        scratch_shapes=[pltpu.VMEM((tm, d), BF16), pltpu.VMEM((tm, LANES), F32),
                        pltpu.VMEM((tm, d), F32)],
        compiler_params=pltpu.CompilerParams(dimension_semantics=("arbitrary", "arbitrary"),
                                             vmem_limit_bytes=V7X_VMEM_LIMIT),
        name="ffn_moe",
    )(x, yc, at, *consts, lw["eg"], lw["eu"], lw["ed"])


def _block_diag_mean(n, group, valid=None):
    m = np.zeros((n, n), np.float32)
    for s in range(0, n if valid is None else valid, group):
        m[s:s + group, s:s + group] = 1.0 / group
    return jnp.asarray(m, BF16)


def _placement_matrices():
    s1 = np.zeros((N_IDX_HEADS * IDX_DIM, N_IDX_HEADS * LANES), np.float32)
    s2 = np.zeros_like(s1)
    t1 = np.zeros((LANES, LANES), np.float32)
    t2 = np.zeros_like(t1)
    j = np.arange(IDX_DIM)
    for h in range(N_IDX_HEADS):
        s1[h * IDX_DIM + j, h * LANES + j] = 1.0
        s2[h * IDX_DIM + j, h * LANES + IDX_DIM + j] = 1.0
        s1[h * IDX_DIM + j, h * LANES + 2 * IDX_DIM + j] = 1.0
    t1[j, j] = 1.0
    t1[j, IDX_DIM + j] = 1.0
    t2[j, 2 * IDX_DIM + j] = 1.0
    return tuple(jnp.asarray(a, BF16) for a in (s1, s2, t1, t2))


def _layer_weights(l, p):
    d = p["w_in"].shape[1]
    s1, s2, t1, t2 = _placement_matrices()
    lw = {
        "attn_g": p["attn_norm_g"][l][None],
        "w_in": jnp.pad(p["w_in"][l], ((0, 0), (0, IN_COLS_PAD - IN_COLS))).astype(BF16),
        "conv_w": p["conv_w"][l],
        "gq": jnp.tile(p["q_norm_g"][l], N_HEADS)[None],
        "gk": jnp.tile(p["k_norm_g"][l], N_KV)[None],
        "giq": jnp.tile(p["iq_norm_g"][l], N_IDX_HEADS)[None],
        "gik": jnp.pad(p["ik_norm_g"][l], (0, LANES - IDX_DIM))[None],
        "mq": _block_diag_mean(ATTN_DIM, HEAD_DIM),
        "mk": _block_diag_mean(N_KV * HEAD_DIM, HEAD_DIM),
        "miq": _block_diag_mean(N_IDX_HEADS * IDX_DIM, IDX_DIM),
        "mik": _block_diag_mean(LANES, IDX_DIM, valid=IDX_DIM),
        "s1": s1, "s2": s2, "t1": t1, "t2": t2,
        "wo_a": p["w_out"][l][:CONV_DIM].astype(BF16),
        "wo_b": p["w_out"][l][CONV_DIM:].astype(BF16),
        "ffn_g": p["ffn_norm_g"][l][None],
    }
    j = l // 2
    if l % 2 == 0:
        lw.update(wg=p["ffn_w_gate"][j].astype(BF16), wu=p["ffn_w_up"][j].astype(BF16),
                  wd=p["ffn_w_down"][j].astype(BF16))
    else:
        r = jnp.pad(p["router_w"][j], ((0, 0), (0, LANES - N_EXPERTS)))
        r_hi = r.astype(BF16)
        lw.update(r_hi=r_hi, r_lo=(r - r_hi.astype(F32)).astype(BF16),
                  eg=p["moe_w_gate"][j].astype(BF16), eu=p["moe_w_up"][j].astype(BF16),
                  ed=p["moe_w_down"][j].astype(BF16))
    assert d == lw["wo_a"].shape[1]
    return lw


def _pick_tile(n, pref):
    t = min(n, pref)
    while n % t:
        t //= 2
    return t


def _mixer(x, prev, lw, *, tq, past_k2=None, past_v=None, past_ik=None, q_pos0=0):
    b, t, d = x.shape
    yc, q, k, v, k2, vb, iqcat, iw, ik, ikcat, cst = _inproj(x, prev, lw, _pick_tile(t, 512))
    if past_k2 is not None:
        k2 = jnp.concatenate([past_k2, k2], axis=1)
        vb = jnp.concatenate([past_v, vb], axis=1)
        ikcat = jnp.concatenate([past_ik, ikcat], axis=1)
    n_keys = k2.shape[1]
    pad = (-n_keys) % (2 * KEY_CHUNK)
    if pad:
        k2, vb, ikcat = (jnp.pad(a, ((0, 0), (0, pad), (0, 0))) for a in (k2, vb, ikcat))
    at = _attention(q, iqcat, iw, k2, vb, ikcat, tq=tq, n_keys=n_keys, q_pos0=q_pos0)
    m = b * t
    x2, yc2, at2 = x.reshape(m, d), yc.reshape(m, CONV_DIM), at.reshape(m, ATTN_DIM)
    if "wg" in lw:
        y = _ffn_dense(x2, yc2, at2, lw, _pick_tile(m, 512))
    else:
        y = _ffn_moe(x2, yc2, at2, lw, _pick_tile(m, 512))
    return (y.reshape(b, t, d), k.reshape(b, t, N_KV, HEAD_DIM), v.reshape(b, t, N_KV, HEAD_DIM),
            ik, cst)


def kernel(x_prompt, x_sample, cache_k, cache_v, cache_ik, state_conv, attn_norm_g, w_in, conv_w,
           q_norm_g, k_norm_g, iq_norm_g, ik_norm_g, w_out, ffn_norm_g, ffn_w_gate, ffn_w_up,
           ffn_w_down, router_w, moe_w_gate, moe_w_up, moe_w_down):
    params = dict(attn_norm_g=attn_norm_g, w_in=w_in, conv_w=conv_w, q_norm_g=q_norm_g,
                  k_norm_g=k_norm_g, iq_norm_g=iq_norm_g, ik_norm_g=ik_norm_g, w_out=w_out,
                  ffn_norm_g=ffn_norm_g, ffn_w_gate=ffn_w_gate, ffn_w_up=ffn_w_up,
                  ffn_w_down=ffn_w_down, router_w=router_w, moe_w_gate=moe_w_gate,
                  moe_w_up=moe_w_up, moe_w_down=moe_w_down)
    depth = w_in.shape[0]
    xp, xs = x_prompt, x_sample
    bp = xp.shape[0]
    bs, ts, _ = xs.shape
    past = cache_k.shape[2]
    kvd = N_KV * HEAD_DIM
    outs = [[] for _ in range(8)]
    for l in range(depth):
        lw = _layer_weights(l, params)
        prev0 = jnp.zeros((bp, CONV_W - 1, CONV_DIM), F32)
        xp, kp, vp, ikp, cp = _mixer(xp, prev0, lw, tq=Q_BLOCK)

        ck = cache_k[l].reshape(bs, past, kvd)
        g0, g1 = ck[..., :HEAD_DIM], ck[..., HEAD_DIM:]
        past_k2 = jnp.concatenate([g0, g0, g1, g1], axis=-1).astype(BF16)
        past_v = cache_v[l].reshape(bs, past, kvd).astype(BF16)
        cik = cache_ik[l]
        cik_hi = cik.astype(BF16)
        cik_lo = (cik - cik_hi.astype(F32)).astype(BF16)
        past_ik = jnp.concatenate([cik_hi, cik_hi, cik_lo, jnp.zeros_like(cik_hi)], axis=-1)
        xs, ks, vs, iks, cs = _mixer(xs, state_conv[l], lw, tq=ts, past_k2=past_k2,
                                     past_v=past_v, past_ik=past_ik, q_pos0=past)
        for lst, val in zip(outs, (kp, vp, ikp, cp, ks, vs, iks, cs)):
            lst.append(val)
    kp, vp, ikp, cp, ks, vs, iks, cs = (jnp.stack(o) for o in outs)
    return (xp, xs, kp, vp, ikp, cp, ks, vs, iks, cs)
```

```python
import functools
import math

import numpy as np
import jax
import jax.numpy as jnp
from jax import lax
from jax.experimental import pallas as pl
from jax.experimental.pallas import tpu as pltpu

CHUNK = 64
CONV_DIM = 512
CONV_W = 3
N_HEADS = 8
HEAD_DIM = 64
ATTN_DIM = N_HEADS * HEAD_DIM
N_KV = 2
N_IDX_HEADS = 8
IDX_DIM = 32
TOPK_MAX = 256
Q_BLOCK = 128
N_EXPERTS = 8
EPS = 1e-6
COL_SIZES = (CONV_DIM, CONV_DIM, CONV_DIM, ATTN_DIM, N_KV * HEAD_DIM, N_KV * HEAD_DIM,
             N_IDX_HEADS * IDX_DIM, IDX_DIM, N_IDX_HEADS)
IN_COLS = sum(COL_SIZES)

LANES = 128
V7X_VMEM_LIMIT = 56 * 1024 * 1024

IN_COLS_PAD = ((IN_COLS + LANES - 1) // LANES) * LANES
TAIL_OFF = IN_COLS_PAD - LANES
IW_LANE = IDX_DIM
QSCALE = HEAD_DIM ** -0.5 * math.log2(math.e)
IWSCALE = IDX_DIM ** -0.5 * N_IDX_HEADS ** -0.5
NEG = -1e30
INT_MIN = -2 ** 31
INT_MAX = 2 ** 31 - 1
KEY_CHUNK = 512
BISECT_ROUND = 4
BISECT_MAX_ROUNDS = -(-(32 + 3) // BISECT_ROUND)
FIRST_PROBE_GAP = 1 << 24

F32 = jnp.float32
BF16 = jnp.bfloat16
I32 = jnp.int32


def _split_hi_lo(x):
    hi = x.astype(BF16)
    lo = (x - hi.astype(F32)).astype(BF16)
    return hi, lo


def _group_rsqrt(x, g_ref):
    hi, lo = _split_hi_lo(x * x)
    g = g_ref[...]
    ms = jnp.dot(hi, g, preferred_element_type=F32) + jnp.dot(lo, g, preferred_element_type=F32)
    return lax.rsqrt(ms + EPS)


def _rmsnorm(x, g):
    return x * lax.rsqrt(jnp.mean(x * x, axis=-1, keepdims=True) + EPS) * g


def _inproj_kernel(x_ref, prev_ref, g_ref, w_ref, cw_ref, gq_ref, gk_ref, giq_ref, gik_ref,
                   mq_ref, mk_ref, miq_ref, mik_ref, s1_ref, s2_ref, t1_ref, t2_ref,
                   yconv_ref, q_ref, k_ref, v_ref, k2_ref, vb_ref, iqcat_ref, iw_ref, ik_ref,
                   ikcat_ref, cst_ref, carry_ref):
    t = pl.program_id(1)
    nt = pl.num_programs(1)
    x = x_ref[0]
    tm = x.shape[0]
    h = _rmsnorm(x, g_ref[...]).astype(BF16)
    p = jnp.dot(h, w_ref[...], preferred_element_type=F32)
    o = np.cumsum((0,) + COL_SIZES)
    c_gate, b_gate, u = p[:, o[0]:o[1]], p[:, o[1]:o[2]], p[:, o[2]:o[3]]
    q, k, v, iq = p[:, o[3]:o[4]], p[:, o[4]:o[5]], p[:, o[5]:o[6]], p[:, o[6]:o[7]]
    tail = p[:, TAIL_OFF:]

    ci = c_gate * u

    @pl.when(t == 0)
    def _():
        carry_ref[8 - (CONV_W - 1):, :] = prev_ref[0]

    row = lax.broadcasted_iota(I32, ci.shape, 0)
    c7 = carry_ref[7:8, :]
    c6 = carry_ref[6:7, :]
    s1 = jnp.where(row == 0, c7, pltpu.roll(ci, 1, axis=0))
    s2 = jnp.where(row == 0, c6, jnp.where(row == 1, c7, pltpu.roll(ci, 2, axis=0)))
    cw = cw_ref[...]
    conv = cw[0:1] * s2 + cw[1:2] * s1 + cw[2:3] * ci
    yconv_ref[0] = (b_gate * conv).astype(BF16)
    carry_ref[...] = ci[tm - 8:, :]

    @pl.when(t == nt - 1)
    def _():
        cst_ref[0] = ci[tm - (CONV_W - 1):, :]

    qn = q * _group_rsqrt(q, mq_ref) * gq_ref[...]
    q_ref[0] = (qn * QSCALE).astype(BF16)
    kn = k * _group_rsqrt(k, mk_ref) * gk_ref[...]
    k_ref[0] = kn
    v_ref[0] = v
    lane = lax.broadcasted_iota(I32, kn.shape, 1)
    kr = pltpu.roll(kn, HEAD_DIM, axis=1)
    k2_ref[0, :, 0:LANES] = jnp.where(lane < HEAD_DIM, kn, kr).astype(BF16)
    k2_ref[0, :, LANES:2 * LANES] = jnp.where(lane < HEAD_DIM, kr, kn).astype(BF16)
    vb_ref[0] = v.astype(BF16)

    iqn = iq * _group_rsqrt(iq, miq_ref) * giq_ref[...]
    ihi, ilo = _split_hi_lo(iqn)
    iqcat_ref[0] = (jnp.dot(ihi, s1_ref[...], preferred_element_type=F32)
                    + jnp.dot(ilo, s2_ref[...], preferred_element_type=F32)).astype(BF16)
    tn = tail * _group_rsqrt(tail, mik_ref) * gik_ref[...]
    ik_ref[0] = tn[:, 0:IDX_DIM]
    thi, tlo = _split_hi_lo(tn)
    ikcat_ref[0] = (jnp.dot(thi, t1_ref[...], preferred_element_type=F32)
                    + jnp.dot(tlo, t2_ref[...], preferred_element_type=F32)).astype(BF16)
    iw_ref[0] = tail * IWSCALE


def _const_spec(shape):
    nd = len(shape)
    return pl.BlockSpec(shape, lambda *_: (0,) * nd)


def _inproj(x, prev, lw, tm):
    b, t, d = x.shape
    grid = (b, t // tm)
    tile = lambda n: pl.BlockSpec((1, tm, n), lambda i, j: (i, j, 0))
    consts = [lw["attn_g"], lw["w_in"], lw["conv_w"], lw["gq"], lw["gk"], lw["giq"], lw["gik"],
              lw["mq"], lw["mk"], lw["miq"], lw["mik"], lw["s1"], lw["s2"], lw["t1"], lw["t2"]]
    in_specs = ([tile(d), pl.BlockSpec((1, CONV_W - 1, CONV_DIM), lambda i, j: (i, 0, 0))]
                + [_const_spec(c.shape) for c in consts])
    kvd = N_KV * HEAD_DIM
    out_shape = (
        jax.ShapeDtypeStruct((b, t, CONV_DIM), BF16),
        jax.ShapeDtypeStruct((b, t, ATTN_DIM), BF16),
        jax.ShapeDtypeStruct((b, t, kvd), F32),
        jax.ShapeDtypeStruct((b, t, kvd), F32),
        jax.ShapeDtypeStruct((b, t, 2 * kvd), BF16),
        jax.ShapeDtypeStruct((b, t, kvd), BF16),
        jax.ShapeDtypeStruct((b, t, N_IDX_HEADS * LANES), BF16),
        jax.ShapeDtypeStruct((b, t, LANES), F32),
        jax.ShapeDtypeStruct((b, t, IDX_DIM), F32),
        jax.ShapeDtypeStruct((b, t, LANES), BF16),
        jax.ShapeDtypeStruct((b, CONV_W - 1, CONV_DIM), F32),
    )
    out_specs = (tile(CONV_DIM), tile(ATTN_DIM), tile(kvd), tile(kvd), tile(2 * kvd), tile(kvd),
                 tile(N_IDX_HEADS * LANES), tile(LANES), tile(IDX_DIM), tile(LANES),
                 pl.BlockSpec((1, CONV_W - 1, CONV_DIM), lambda i, j: (i, 0, 0)))
    return pl.pallas_call(
        _inproj_kernel, grid=grid, in_specs=in_specs, out_specs=out_specs, out_shape=out_shape,
        scratch_shapes=[pltpu.VMEM((8, CONV_DIM), F32)],
        compiler_params=pltpu.CompilerParams(dimension_semantics=("arbitrary", "arbitrary"),
                                             vmem_limit_bytes=V7X_VMEM_LIMIT),
        name="inproj",
    )(x, prev, *consts)


def _slabs(n):
    return [slice(j * LANES, (j + 1) * LANES) for j in range(n // LANES)]


def _tree(op, xs):
    while len(xs) > 1:
        xs = [op(xs[i], xs[i + 1]) for i in range(0, len(xs) - 1, 2)] + ([xs[-1]] if len(xs) % 2 else [])
    return xs[0]


def _attn_kernel(q_ref, iqcat_ref, iw_ref, k2_ref, v_ref, ikcat_ref, tri_ref, o_ref,
                 keyst_ref, keys_ref, lga_ref, lgb_ref, lhs_ref, wb_ref, qg_ref, sa_ref, sb_ref,
                 biasa_ref, biasb_ref, mxa_ref, mxb_ref, p_ref, m_ref, l_ref, acc_ref,
                 *, tq, kc, n_keys, q_pos0, topk):
    qi = pl.program_id(1)
    q0 = q_pos0 + qi * tq
    shift = CHUNK.bit_length() - 1

    def key_limit(pos):
        return jnp.minimum(((pos >> shift) + 1) << shift, n_keys)

    lane = lax.broadcasted_iota(I32, (tq, LANES), 1)
    lim = key_limit(q0 + lax.broadcasted_iota(I32, (tq, LANES), 0))
    qlane = lax.broadcasted_iota(I32, (1, LANES), 1)
    lim_t = jnp.where(qlane < tq, key_limit(q0 + qlane), 0)
    n_chunks = (key_limit(q0 + tq - 1) + kc - 1) // kc
    kfl = jnp.minimum(lim_t, topk).astype(F32)

    iqc = iqcat_ref[0]
    for h in range(N_IDX_HEADS):
        lhs_ref[h * tq:(h + 1) * tq, :] = iqc[:, h * LANES:(h + 1) * LANES]
    iw = iw_ref[0]
    for h in range(N_IDX_HEADS):
        col = jnp.sum(jnp.where(lane == IW_LANE + h, iw, 0.0), axis=1, keepdims=True)
        wb_ref[h] = jnp.broadcast_to(col, (tq, LANES))
    pad_rows = jnp.full((LANES - tq, LANES), INT_MIN, I32) if tq < LANES else None

    last_chunk = keys_ref.shape[0] - 1

    def logits(c, lg_ref):
        off = pl.multiple_of(jnp.minimum(c, last_chunk) * kc, kc)
        lg_ref[...] = lax.dot_general(lhs_ref[...], ikcat_ref[0, pl.ds(off, kc), :],
                                      (((1,), (1,)), ((), ())), preferred_element_type=F32)

    def keys_from(c, lg_ref, kmax):
        for j, sl in enumerate(_slabs(kc)):
            sc = None
            for h in range(N_IDX_HEADS):
                term = jnp.maximum(lg_ref[h * tq:(h + 1) * tq, sl], 0.0) * wb_ref[h]
                sc = term if sc is None else sc + term
            sc = sc + 0.0
            bits = lax.bitcast_convert_type(sc, I32)
            key = bits ^ ((bits >> 31) & INT_MAX)
            key = jnp.where(c * kc + j * LANES + lane < lim, key, INT_MIN)
            keys_ref[c, :, sl] = key
            if pad_rows is not None:
                key = jnp.concatenate([key, pad_rows], axis=0)
            kt = key.T
            keyst_ref[c, sl, :] = kt
            kmax = jnp.maximum(kmax, _tree(jnp.maximum, [kt[8 * i:8 * i + 8, :]
                                                         for i in range(LANES // 8)]))
        return kmax

    logits(0, lga_ref)

    def score_pair(i, kmax):
        logits(2 * i + 1, lgb_ref)
        kmax = keys_from(2 * i, lga_ref, kmax)
        logits(2 * i + 2, lga_ref)
        return keys_from(2 * i + 1, lgb_ref, kmax)

    kmax = lax.fori_loop(0, (n_chunks + 1) // 2, score_pair, jnp.full((8, LANES), INT_MIN, I32))
    kmax = jnp.max(kmax, axis=0, keepdims=True)

    tree_sum = functools.partial(_tree, jnp.add)

    def count_ge(thr):
        def body(c, acc):
            hit = jnp.where(keyst_ref[c] >= thr, 1.0, 0.0)
            return acc + tree_sum([hit[8 * i:8 * i + 8, :] for i in range(kc // 8)])

        part = lax.fori_loop(0, n_chunks, body, jnp.zeros((8, LANES), F32))
        return jnp.sum(part, axis=0, keepdims=True)

    lo0 = jnp.full((1, LANES), INT_MIN + 1, I32)

    def next_probe(lo, hi):
        mid = (lo & hi) + ((lo ^ hi) >> 1)
        floor = lo == lo0
        return jnp.where(jnp.logical_and(floor, hi > 1), 1,
                         jnp.where(jnp.logical_and(floor, hi == 1), 0, mid))

    def bisect_pass(lo, hi, clo, chi, mid):
        cnt = count_ge(mid)
        ge = cnt >= kfl
        exact = cnt == kfl
        lo_n = jnp.where(ge, mid, lo)
        hi_n = jnp.where(exact, mid + 1, jnp.where(ge, hi, mid))
        return lo_n, hi_n, jnp.where(ge, cnt, clo), jnp.where(ge, chi, cnt)

    def bisect_cond(carry):
        it, pending = carry[0], carry[1]
        return jnp.logical_and(pending > 0.0, it < BISECT_MAX_ROUNDS)

    def bisect_body(carry):
        it, _, lo, hi, clo, chi = carry
        for _ in range(BISECT_ROUND):
            lo, hi, clo, chi = bisect_pass(lo, hi, clo, chi, next_probe(lo, hi))
        pending = jnp.max(jnp.where(hi != lo + 1, 1.0, 0.0), axis=1, keepdims=True)[0, 0]
        return (it + 1, pending, lo, hi, clo, chi)

    hi = jnp.maximum(jnp.minimum(kmax, INT_MAX - 1) + 1, lo0 + 1)
    first = jnp.where(kmax > INT_MIN + 2 + FIRST_PROBE_GAP, kmax - FIRST_PROBE_GAP, next_probe(lo0, hi))
    state = bisect_pass(lo0, hi, lim_t.astype(F32), jnp.zeros((1, LANES), F32), first)
    _, _, thr, _, clo, chi = lax.while_loop(bisect_cond, bisect_body,
                                            (jnp.int32(0), jnp.float32(1.0)) + state)

    extra = clo - kfl
    any_tie = jnp.max(extra, axis=1, keepdims=True)[0, 0]

    @pl.when(any_tie > 0.0)
    def _():
        need = jnp.where(extra > 0.0, kfl - chi, jnp.inf)

        def demote(c, seen):
            kch = keyst_ref[c]
            tied = kch == thr
            rank = seen + jnp.dot(tri_ref[...], jnp.where(tied, 1.0, 0.0).astype(BF16),
                                  preferred_element_type=F32)
            kch = jnp.where(jnp.logical_and(tied, rank > need), kch - 1, kch)
            keyst_ref[c] = kch
            for sl in _slabs(kc):
                keys_ref[c, :, sl] = kch[sl, :].T[:tq, :]
            return rank[kc - 1:kc, :]

        lax.fori_loop(0, n_chunks, demote, jnp.zeros((1, LANES), F32))

    thr_rows = jnp.broadcast_to(thr, (LANES, LANES)).T[:tq, :]
    qv = q_ref[0].astype(F32)
    heads_per_group = N_HEADS // N_KV
    rows = heads_per_group * tq
    for g in range(N_KV):
        for r in range(heads_per_group):
            hd = g * heads_per_group + r
            slab = qv[:, (hd // 2) * LANES:(hd // 2 + 1) * LANES]
            keep = (lane < HEAD_DIM) if hd % 2 == 0 else (lane >= HEAD_DIM)
            qg_ref[g, r * tq:(r + 1) * tq, :] = jnp.where(keep, slab, 0.0).astype(BF16)
    m_ref[...] = jnp.full(m_ref.shape, NEG, F32)
    l_ref[...] = jnp.zeros(l_ref.shape, F32)
    acc_ref[...] = jnp.zeros(acc_ref.shape, F32)

    def qk_scores(c, s_ref, bias_ref, mx_ref):
        c = jnp.minimum(c, last_chunk)
        off = pl.multiple_of(c * kc, kc)
        for sl in _slabs(kc):
            bias_ref[:, sl] = jnp.where(keys_ref[c, :, sl] >= thr_rows, 0.0, NEG)
        for g in range(N_KV):
            kch = k2_ref[0, pl.ds(off, kc), g * LANES:(g + 1) * LANES]
            s = lax.dot_general(qg_ref[g], kch, (((1,), (1,)), ((), ())),
                                preferred_element_type=F32)
            s_ref[g] = s
            for r in range(heads_per_group):
                rs = slice(r * tq, (r + 1) * tq)
                part = _tree(jnp.maximum, [s[rs, sl] + bias_ref[:, sl] for sl in _slabs(kc)])
                mx_ref[g, rs, :] = jnp.broadcast_to(jnp.max(part, axis=1, keepdims=True), (tq, LANES))

    def softmax_pv(c, s_ref, bias_ref, mx_ref):
        off = pl.multiple_of(c * kc, kc)
        vch = v_ref[0, pl.ds(off, kc), :]
        for g in range(N_KV):
            m_prev = m_ref[g]
            m_new = jnp.maximum(m_prev, mx_ref[g])
            alpha = jnp.exp2(m_prev - m_new)
            for r in range(heads_per_group):
                rs = slice(r * tq, (r + 1) * tq)
                ps = [jnp.exp2(s_ref[g, rs, sl] + bias_ref[:, sl] - m_new[rs]) for sl in _slabs(kc)]
                p_ref[g, rs, :] = jnp.concatenate(ps, axis=1).astype(BF16)
                l_ref[g, rs, :] = alpha[rs] * l_ref[g, rs, :] + tree_sum(ps)
            acc_ref[g] = alpha * acc_ref[g] + jnp.dot(p_ref[g], vch, preferred_element_type=F32)
            m_ref[g] = m_new

    stage_a = (sa_ref, biasa_ref, mxa_ref)
    stage_b = (sb_ref, biasb_ref, mxb_ref)
    qk_scores(0, *stage_a)

    def attend_pair(i, carry):
        qk_scores(2 * i + 1, *stage_b)
        softmax_pv(2 * i, *stage_a)
        qk_scores(2 * i + 2, *stage_a)
        softmax_pv(2 * i + 1, *stage_b)
        return carry

    lax.fori_loop(0, (n_chunks + 1) // 2, attend_pair, 0)

    for g in range(N_KV):
        out = acc_ref[g] / jnp.sum(l_ref[g], axis=1, keepdims=True)
        for jj in range(heads_per_group // 2):
            a = out[(2 * jj) * tq:(2 * jj + 1) * tq, :]
            bb = out[(2 * jj + 1) * tq:(2 * jj + 2) * tq, :]
            if g % 2 == 0:
                slab = jnp.where(lane < HEAD_DIM, a, pltpu.roll(bb, HEAD_DIM, axis=1))
            else:
                slab = jnp.where(lane < HEAD_DIM, pltpu.roll(a, HEAD_DIM, axis=1), bb)
            so = (g * heads_per_group // 2 + jj) * LANES
            o_ref[0, :, so:so + LANES] = slab.astype(BF16)


def _attention(q, iqcat, iw, k2, vb, ikcat, *, tq, n_keys, q_pos0):
    b, t, _ = q.shape
    lpad = k2.shape[1]
    kc = KEY_CHUNK
    assert lpad % (2 * kc) == 0 and t % tq == 0 and tq <= LANES and CHUNK & (CHUNK - 1) == 0
    topk = min(TOPK_MAX, n_keys // 4)
    kern = functools.partial(_attn_kernel, tq=tq, kc=kc, n_keys=n_keys, q_pos0=q_pos0, topk=topk)
    qtile = lambda n: pl.BlockSpec((1, tq, n), lambda i, j: (i, j, 0))
    ktile = lambda n: pl.BlockSpec((1, lpad, n), lambda i, j: (i, 0, 0))
    rows = (N_HEADS // N_KV) * tq
    return pl.pallas_call(
        kern, grid=(b, t // tq),
        in_specs=[qtile(ATTN_DIM), qtile(N_IDX_HEADS * LANES), qtile(LANES),
                  ktile(2 * N_KV * HEAD_DIM), ktile(N_KV * HEAD_DIM), ktile(LANES),
                  pl.BlockSpec((kc, kc), lambda i, j: (0, 0))],
        out_specs=qtile(ATTN_DIM),
        out_shape=jax.ShapeDtypeStruct((b, t, ATTN_DIM), BF16),
        scratch_shapes=[
            pltpu.VMEM((lpad // kc, kc, LANES), I32),
            pltpu.VMEM((lpad // kc, tq, kc), I32),
            pltpu.VMEM((N_IDX_HEADS * tq, kc), F32),
            pltpu.VMEM((N_IDX_HEADS * tq, kc), F32),
            pltpu.VMEM((N_IDX_HEADS * tq, LANES), BF16),
            pltpu.VMEM((N_IDX_HEADS, tq, LANES), F32),
            pltpu.VMEM((N_KV, rows, LANES), BF16),
            pltpu.VMEM((N_KV, rows, kc), F32),
            pltpu.VMEM((N_KV, rows, kc), F32),
            pltpu.VMEM((tq, kc), F32),
            pltpu.VMEM((tq, kc), F32),
            pltpu.VMEM((N_KV, rows, LANES), F32),
            pltpu.VMEM((N_KV, rows, LANES), F32),
            pltpu.VMEM((N_KV, rows, kc), BF16),
            pltpu.VMEM((N_KV, rows, LANES), F32),
            pltpu.VMEM((N_KV, rows, LANES), F32),
            pltpu.VMEM((N_KV, rows, LANES), F32),
        ],
        compiler_params=pltpu.CompilerParams(dimension_semantics=("arbitrary", "arbitrary"),
                                             vmem_limit_bytes=V7X_VMEM_LIMIT),
        name="attention",
    )(q, iqcat, iw, k2, vb, ikcat, jnp.asarray(np.tril(np.ones((kc, kc), np.float32)), BF16))


def _mix_in(x_ref, yc_ref, at_ref, woa_ref, wob_ref, g_ref):
    x1 = (x_ref[...] + jnp.dot(yc_ref[...], woa_ref[...], preferred_element_type=F32)
          + jnp.dot(at_ref[...], wob_ref[...], preferred_element_type=F32))
    return x1, _rmsnorm(x1, g_ref[...])


def _swiglu(hb, wg, wu, wd):
    a = jnp.dot(hb, wg, preferred_element_type=F32)
    b = jnp.dot(hb, wu, preferred_element_type=F32)
    act = (a * (1.0 / (1.0 + jnp.exp(-a))) * b).astype(BF16)
    return jnp.dot(act, wd, preferred_element_type=F32)


def _ffn_dense_kernel(x_ref, yc_ref, at_ref, woa_ref, wob_ref, g_ref, wg_ref, wu_ref, wd_ref, o_ref):
    x1, hn = _mix_in(x_ref, yc_ref, at_ref, woa_ref, wob_ref, g_ref)
    o_ref[...] = x1 + _swiglu(hn.astype(BF16), wg_ref[...], wu_ref[...], wd_ref[...])


def _ffn_dense(x, yc, at, lw, tm):
    m, d = x.shape
    row = lambda n: pl.BlockSpec((tm, n), lambda i: (i, 0))
    consts = [lw["wo_a"], lw["wo_b"], lw["ffn_g"], lw["wg"], lw["wu"], lw["wd"]]
    return pl.pallas_call(
        _ffn_dense_kernel, grid=(m // tm,),
        in_specs=[row(d), row(CONV_DIM), row(ATTN_DIM)]
        + [pl.BlockSpec(c.shape, lambda i: (0, 0), pipeline_mode=pl.Buffered(1)) for c in consts],
        out_specs=row(d), out_shape=jax.ShapeDtypeStruct((m, d), F32),
        compiler_params=pltpu.CompilerParams(dimension_semantics=("arbitrary",),
                                             vmem_limit_bytes=V7X_VMEM_LIMIT),
        name="ffn_dense",
    )(x, yc, at, *consts)


def _ffn_moe_kernel(x_ref, yc_ref, at_ref, woa_ref, wob_ref, g_ref, rhi_ref, rlo_ref, tril_ref,
                    eg_ref, eu_ref, ed_ref, o_ref, h_ref, gate_ref, rank_ref, rankt_ref, cnt_ref, acc_ref,
                    *, cap):
    e = pl.program_id(1)
    tm = x_ref.shape[0]
    lane = lax.broadcasted_iota(I32, (tm, LANES), 1)

    @pl.when(e == 0)
    def _():
        x1, hn = _mix_in(x_ref, yc_ref, at_ref, woa_ref, wob_ref, g_ref)
        hi, lo = _split_hi_lo(hn)
        h_ref[...] = hi
        rhi = rhi_ref[...]
        logits = (jnp.dot(hi, rhi, preferred_element_type=F32)
                  + jnp.dot(lo, rhi, preferred_element_type=F32)
                  + jnp.dot(hi, rlo_ref[...], preferred_element_type=F32))
        logits = jnp.where(lane < N_EXPERTS, logits, -jnp.inf)
        ex = jnp.exp(logits - jnp.max(logits, axis=1, keepdims=True))
        probs = ex / jnp.sum(ex, axis=1, keepdims=True)
        p1 = jnp.max(probs, axis=1, keepdims=True)
        i1 = jnp.min(jnp.where(probs == p1, lane, LANES), axis=1, keepdims=True)
        rest = jnp.where(lane == i1, -1.0, probs)
        p2 = jnp.max(rest, axis=1, keepdims=True)
        i2 = jnp.min(jnp.where(rest == p2, lane, LANES), axis=1, keepdims=True)
        den = p1 + p2
        gate_ref[...] = jnp.where(lane == i1, p1 / den, jnp.where(lane == i2, p2 / den, 0.0))
        routed = jnp.where(jnp.logical_or(lane == i1, lane == i2), 1.0, 0.0)
        before = jnp.dot(tril_ref[...], routed.astype(BF16), preferred_element_type=F32)
        rank = jnp.where(routed > 0.0, before, -1.0)
        rank_ref[...] = rank
        for r0 in range(0, tm, LANES):
            nr = min(LANES, tm - r0)
            blk = rank[r0:r0 + nr, :]
            if nr < LANES:
                blk = jnp.concatenate([blk, jnp.full((LANES - nr, LANES), -1.0, F32)], axis=0)
            rt = blk.T
            for ex_id in range(N_EXPERTS):
                rankt_ref[ex_id, :, r0:r0 + nr] = rt[ex_id:ex_id + 1, :nr]
        cnt_ref[...] = jnp.broadcast_to(jnp.sum(routed, axis=0, keepdims=True), cnt_ref.shape)
        acc_ref[...] = x1

    mine = lane == e
    rank_col = jnp.sum(jnp.where(mine, rank_ref[...], 0.0), axis=1, keepdims=True)
    gate_col = jnp.sum(jnp.where(mine, gate_ref[...], 0.0), axis=1, keepdims=True)
    rank_row = rankt_ref[e]
    n_tok = jnp.sum(jnp.where(mine[:8], cnt_ref[...], 0.0), axis=1, keepdims=True)[0, 0]
    n_sub = (n_tok.astype(I32) + cap - 1) // cap
    kpad = -(-cap // LANES) * LANES
    slot_r = lax.broadcasted_iota(I32, (cap, tm), 0).astype(F32)
    slot_c = lax.broadcasted_iota(I32, (tm, kpad), 1).astype(F32)

    def sub_block(sb, carry):
        base = (sb * cap).astype(F32)
        gather = jnp.where(rank_row - base == slot_r, 1.0, 0.0).astype(BF16)
        hs = jnp.dot(gather, h_ref[...], preferred_element_type=F32).astype(BF16)
        y = _swiglu(hs, eg_ref[0], eu_ref[0], ed_ref[0]).astype(BF16)
        if kpad > cap:
            y = jnp.concatenate([y, jnp.zeros((kpad - cap, y.shape[1]), BF16)], axis=0)
        scatter = jnp.where(rank_col - base == slot_c, 1.0, 0.0).astype(BF16)
        acc_ref[...] += gate_col * jnp.dot(scatter, y, preferred_element_type=F32)
        return carry

    lax.fori_loop(0, n_sub, sub_block, 0)

    @pl.when(e == pl.num_programs(1) - 1)
    def _():
        o_ref[...] = acc_ref[...]


def _ffn_moe(x, yc, at, lw, tm):
    m, d = x.shape
    ne, _, fe = lw["eg"].shape
    cap = min(tm, -(-(9 * tm // 32) // 16) * 16)
    row = lambda n: pl.BlockSpec((tm, n), lambda i, e: (i, 0))
    tril = jnp.asarray(np.tril(np.ones((tm, tm), np.float32), -1), BF16)
    consts = [lw["wo_a"], lw["wo_b"], lw["ffn_g"], lw["r_hi"], lw["r_lo"], tril]
    return pl.pallas_call(
        functools.partial(_ffn_moe_kernel, cap=cap), grid=(m // tm, ne),
        in_specs=[row(d), row(CONV_DIM), row(ATTN_DIM)]
        + [pl.BlockSpec(c.shape, lambda i, e: (0, 0), pipeline_mode=pl.Buffered(1)) for c in consts]
        + [pl.BlockSpec((1, d, fe), lambda i, e: (e, 0, 0)),
           pl.BlockSpec((1, d, fe), lambda i, e: (e, 0, 0)),
           pl.BlockSpec((1, fe, d), lambda i, e: (e, 0, 0))],
        out_specs=row(d), out_shape=jax.ShapeDtypeStruct((m, d), F32),
        scratch_shapes=[pltpu.VMEM((tm, d), BF16),
                        pltpu.VMEM((tm, LANES), F32),
                        pltpu.VMEM((tm, LANES), F32),
                        pltpu.VMEM((ne, 1, tm), F32),
                        pltpu.VMEM((8, LANES), F32),
                        pltpu.VMEM((tm, d), F32)],
        compiler_params=pltpu.CompilerParams(dimension_semantics=("arbitrary", "arbitrary"),
                                             vmem_limit_bytes=V7X_VMEM_LIMIT),
        name="ffn_moe",
    )(x, yc, at, *consts, lw["eg"], lw["eu"], lw["ed"])


def _block_diag_mean(n, group, valid=None):
    m = np.zeros((n, n), np.float32)
    for s in range(0, n if valid is None else valid, group):
        m[s:s + group, s:s + group] = 1.0 / group
    return jnp.asarray(m, BF16)


def _placement_matrices():
    s1 = np.zeros((N_IDX_HEADS * IDX_DIM, N_IDX_HEADS * LANES), np.float32)
    s2 = np.zeros_like(s1)
    t1 = np.zeros((LANES, LANES), np.float32)
    t2 = np.zeros_like(t1)
    j = np.arange(IDX_DIM)
    for h in range(N_IDX_HEADS):
        s1[h * IDX_DIM + j, h * LANES + j] = 1.0
        s2[h * IDX_DIM + j, h * LANES + IDX_DIM + j] = 1.0
        s1[h * IDX_DIM + j, h * LANES + 2 * IDX_DIM + j] = 1.0
    t1[j, j] = 1.0
    t1[j, IDX_DIM + j] = 1.0
    t2[j, 2 * IDX_DIM + j] = 1.0
    return tuple(jnp.asarray(a, BF16) for a in (s1, s2, t1, t2))


def _layer_weights(l, p):
    d = p["w_in"].shape[1]
    s1, s2, t1, t2 = _placement_matrices()
    lw = {
        "attn_g": p["attn_norm_g"][l][None],
        "w_in": jnp.pad(p["w_in"][l], ((0, 0), (0, IN_COLS_PAD - IN_COLS))).astype(BF16),
        "conv_w": p["conv_w"][l],
        "gq": jnp.tile(p["q_norm_g"][l], N_HEADS)[None],
        "gk": jnp.tile(p["k_norm_g"][l], N_KV)[None],
        "giq": jnp.tile(p["iq_norm_g"][l], N_IDX_HEADS)[None],
        "gik": jnp.pad(p["ik_norm_g"][l], (0, LANES - IDX_DIM))[None],
        "mq": _block_diag_mean(ATTN_DIM, HEAD_DIM),
        "mk": _block_diag_mean(N_KV * HEAD_DIM, HEAD_DIM),
        "miq": _block_diag_mean(N_IDX_HEADS * IDX_DIM, IDX_DIM),
        "mik": _block_diag_mean(LANES, IDX_DIM, valid=IDX_DIM),
        "s1": s1, "s2": s2, "t1": t1, "t2": t2,
        "wo_a": p["w_out"][l][:CONV_DIM].astype(BF16),
        "wo_b": p["w_out"][l][CONV_DIM:].astype(BF16),
        "ffn_g": p["ffn_norm_g"][l][None],
    }
    j = l // 2
    if l % 2 == 0:
        lw.update(wg=p["ffn_w_gate"][j].astype(BF16), wu=p["ffn_w_up"][j].astype(BF16),
                  wd=p["ffn_w_down"][j].astype(BF16))
    else:
        r = jnp.pad(p["router_w"][j], ((0, 0), (0, LANES - N_EXPERTS)))
        r_hi = r.astype(BF16)
        lw.update(r_hi=r_hi, r_lo=(r - r_hi.astype(F32)).astype(BF16),
                  eg=p["moe_w_gate"][j].astype(BF16), eu=p["moe_w_up"][j].astype(BF16),
                  ed=p["moe_w_down"][j].astype(BF16))
    assert d == lw["wo_a"].shape[1]
    return lw


def _pick_tile(n, pref):
    t = min(n, pref)
    while n % t:
        t //= 2
    return t


def _mixer(x, prev, lw, *, tq, past_k2=None, past_v=None, past_ik=None, q_pos0=0):
    b, t, d = x.shape
    yc, q, k, v, k2, vb, iqcat, iw, ik, ikcat, cst = _inproj(x, prev, lw, _pick_tile(t, 512))
    if past_k2 is not None:
        k2 = jnp.concatenate([past_k2, k2], axis=1)
        vb = jnp.concatenate([past_v, vb], axis=1)
        ikcat = jnp.concatenate([past_ik, ikcat], axis=1)
    n_keys = k2.shape[1]
    pad = (-n_keys) % (2 * KEY_CHUNK)
    if pad:
        k2, vb, ikcat = (jnp.pad(a, ((0, 0), (0, pad), (0, 0))) for a in (k2, vb, ikcat))
    at = _attention(q, iqcat, iw, k2, vb, ikcat, tq=tq, n_keys=n_keys, q_pos0=q_pos0)
    m = b * t
    x2, yc2, at2 = x.reshape(m, d), yc.reshape(m, CONV_DIM), at.reshape(m, ATTN_DIM)
    if "wg" in lw:
        y = _ffn_dense(x2, yc2, at2, lw, _pick_tile(m, 512))
    else:
        y = _ffn_moe(x2, yc2, at2, lw, _pick_tile(m, 1024))
    return (y.reshape(b, t, d), k.reshape(b, t, N_KV, HEAD_DIM), v.reshape(b, t, N_KV, HEAD_DIM),
            ik, cst)


def kernel(x_prompt, x_sample, cache_k, cache_v, cache_ik, state_conv, attn_norm_g, w_in, conv_w,
           q_norm_g, k_norm_g, iq_norm_g, ik_norm_g, w_out, ffn_norm_g, ffn_w_gate, ffn_w_up,
           ffn_w_down, router_w, moe_w_gate, moe_w_up, moe_w_down):
    params = dict(attn_norm_g=attn_norm_g, w_in=w_in, conv_w=conv_w, q_norm_g=q_norm_g,
                  k_norm_g=k_norm_g, iq_norm_g=iq_norm_g, ik_norm_g=ik_norm_g, w_out=w_out,
                  ffn_norm_g=ffn_norm_g, ffn_w_gate=ffn_w_gate, ffn_w_up=ffn_w_up,
                  ffn_w_down=ffn_w_down, router_w=router_w, moe_w_gate=moe_w_gate,
                  moe_w_up=moe_w_up, moe_w_down=moe_w_down)
    depth = w_in.shape[0]
    xp, xs = x_prompt, x_sample
    bp = xp.shape[0]
    bs, ts, _ = xs.shape
    past = cache_k.shape[2]
    kvd = N_KV * HEAD_DIM
    outs = [[] for _ in range(8)]
    for l in range(depth):
        lw = _layer_weights(l, params)
        prev0 = jnp.zeros((bp, CONV_W - 1, CONV_DIM), F32)
        xp, kp, vp, ikp, cp = _mixer(xp, prev0, lw, tq=Q_BLOCK)

        ck = cache_k[l].reshape(bs, past, kvd)
        g0, g1 = ck[..., :HEAD_DIM], ck[..., HEAD_DIM:]
        past_k2 = jnp.concatenate([g0, g0, g1, g1], axis=-1).astype(BF16)
        past_v = cache_v[l].reshape(bs, past, kvd).astype(BF16)
        cik = cache_ik[l]
        cik_hi = cik.astype(BF16)
        cik_lo = (cik - cik_hi.astype(F32)).astype(BF16)
        past_ik = jnp.concatenate([cik_hi, cik_hi, cik_lo, jnp.zeros_like(cik_hi)], axis=-1)
        xs, ks, vs, iks, cs = _mixer(xs, state_conv[l], lw, tq=ts, past_k2=past_k2,
                                     past_v=past_v, past_ik=past_ik, q_pos0=past)
        for lst, val in zip(outs, (kp, vp, ikp, cp, ks, vs, iks, cs)):
            lst.append(val)
    kp, vp, ikp, cp, ks, vs, iks, cs = (jnp.stack(o) for o in outs)
    return (xp, xs, kp, vp, ikp, cp, ks, vs, iks, cs)
```

```python
import functools
import math

import numpy as np
import jax
import jax.numpy as jnp
from jax import lax
from jax.experimental import pallas as pl
from jax.experimental.pallas import tpu as pltpu

CHUNK = 64
CONV_DIM = 512
CONV_W = 3
N_HEADS = 8
HEAD_DIM = 64
ATTN_DIM = N_HEADS * HEAD_DIM
N_KV = 2
N_IDX_HEADS = 8
IDX_DIM = 32
TOPK_MAX = 256
Q_BLOCK = 128
N_EXPERTS = 8
EPS = 1e-6
COL_SIZES = (CONV_DIM, CONV_DIM, CONV_DIM, ATTN_DIM, N_KV * HEAD_DIM, N_KV * HEAD_DIM,
             N_IDX_HEADS * IDX_DIM, IDX_DIM, N_IDX_HEADS)
IN_COLS = sum(COL_SIZES)

LANES = 128
V7X_VMEM_LIMIT = 56 * 1024 * 1024

IN_COLS_PAD = ((IN_COLS + LANES - 1) // LANES) * LANES
TAIL_OFF = IN_COLS_PAD - LANES
IW_LANE = IDX_DIM
QSCALE = HEAD_DIM ** -0.5 * math.log2(math.e)
IWSCALE = IDX_DIM ** -0.5 * N_IDX_HEADS ** -0.5
NEG = -1e30
INT_MIN = -2 ** 31
INT_MAX = 2 ** 31 - 1
KEY_CHUNK = 512
BISECT_ROUND = 4
EXTRACT_BELOW = 2
BISECT_MAX_ROUNDS = -(-(32 + 3) // BISECT_ROUND) + EXTRACT_BELOW
FIRST_PROBE_GAP = 3 << 23

F32 = jnp.float32
BF16 = jnp.bfloat16
I32 = jnp.int32


def _split_hi_lo(x):
    hi = x.astype(BF16)
    lo = (x - hi.astype(F32)).astype(BF16)
    return hi, lo


def _group_rsqrt(x, g_ref):
    hi, lo = _split_hi_lo(x * x)
    g = g_ref[...]
    ms = jnp.dot(hi, g, preferred_element_type=F32) + jnp.dot(lo, g, preferred_element_type=F32)
    return lax.rsqrt(ms + EPS)


def _rmsnorm(x, g):
    return x * lax.rsqrt(jnp.mean(x * x, axis=-1, keepdims=True) + EPS) * g


def _inproj_kernel(x_ref, prev_ref, g_ref, w_ref, cw_ref, gq_ref, gk_ref, giq_ref, gik_ref,
                   mq_ref, mk_ref, miq_ref, mik_ref, s1_ref, s2_ref, t1_ref, t2_ref,
                   yconv_ref, q_ref, k_ref, v_ref, k2_ref, vb_ref, iqcat_ref, iw_ref, ik_ref,
                   ikcat_ref, cst_ref, carry_ref):
    t = pl.program_id(1)
    nt = pl.num_programs(1)
    x = x_ref[0]
    tm = x.shape[0]
    h = _rmsnorm(x, g_ref[...]).astype(BF16)
    p = jnp.dot(h, w_ref[...], preferred_element_type=F32)
    o = np.cumsum((0,) + COL_SIZES)
    c_gate, b_gate, u = p[:, o[0]:o[1]], p[:, o[1]:o[2]], p[:, o[2]:o[3]]
    q, k, v, iq = p[:, o[3]:o[4]], p[:, o[4]:o[5]], p[:, o[5]:o[6]], p[:, o[6]:o[7]]
    tail = p[:, TAIL_OFF:]

    ci = c_gate * u

    @pl.when(t == 0)
    def _():
        carry_ref[8 - (CONV_W - 1):, :] = prev_ref[0]

    row = lax.broadcasted_iota(I32, ci.shape, 0)
    c7 = carry_ref[7:8, :]
    c6 = carry_ref[6:7, :]
    s1 = jnp.where(row == 0, c7, pltpu.roll(ci, 1, axis=0))
    s2 = jnp.where(row == 0, c6, jnp.where(row == 1, c7, pltpu.roll(ci, 2, axis=0)))
    cw = cw_ref[...]
    conv = cw[0:1] * s2 + cw[1:2] * s1 + cw[2:3] * ci
    yconv_ref[0] = (b_gate * conv).astype(BF16)
    carry_ref[...] = ci[tm - 8:, :]

    @pl.when(t == nt - 1)
    def _():
        cst_ref[0] = ci[tm - (CONV_W - 1):, :]

    qn = q * _group_rsqrt(q, mq_ref) * gq_ref[...]
    q_ref[0] = (qn * QSCALE).astype(BF16)
    kn = k * _group_rsqrt(k, mk_ref) * gk_ref[...]
    k_ref[0] = kn
    v_ref[0] = v
    lane = lax.broadcasted_iota(I32, kn.shape, 1)
    kr = pltpu.roll(kn, HEAD_DIM, axis=1)
    k2_ref[0, :, 0:LANES] = jnp.where(lane < HEAD_DIM, kn, kr).astype(BF16)
    k2_ref[0, :, LANES:2 * LANES] = jnp.where(lane < HEAD_DIM, kr, kn).astype(BF16)
    vb_ref[0] = v.astype(BF16)

    iqn = iq * _group_rsqrt(iq, miq_ref) * giq_ref[...]
    ihi, ilo = _split_hi_lo(iqn)
    iqcat_ref[0] = (jnp.dot(ihi, s1_ref[...], preferred_element_type=F32)
                    + jnp.dot(ilo, s2_ref[...], preferred_element_type=F32)).astype(BF16)
    tn = tail * _group_rsqrt(tail, mik_ref) * gik_ref[...]
    ik_ref[0] = tn[:, 0:IDX_DIM]
    thi, tlo = _split_hi_lo(tn)
    ikcat_ref[0] = (jnp.dot(thi, t1_ref[...], preferred_element_type=F32)
                    + jnp.dot(tlo, t2_ref[...], preferred_element_type=F32)).astype(BF16)
    iw_ref[0] = tail * IWSCALE


def _const_spec(shape):
    nd = len(shape)
    return pl.BlockSpec(shape, lambda *_: (0,) * nd)


def _inproj(x, prev, lw, tm):
    b, t, d = x.shape
    grid = (b, t // tm)
    tile = lambda n: pl.BlockSpec((1, tm, n), lambda i, j: (i, j, 0))
    consts = [lw["attn_g"], lw["w_in"], lw["conv_w"], lw["gq"], lw["gk"], lw["giq"], lw["gik"],
              lw["mq"], lw["mk"], lw["miq"], lw["mik"], lw["s1"], lw["s2"], lw["t1"], lw["t2"]]
    in_specs = ([tile(d), pl.BlockSpec((1, CONV_W - 1, CONV_DIM), lambda i, j: (i, 0, 0))]
                + [_const_spec(c.shape) for c in consts])
    kvd = N_KV * HEAD_DIM
    out_shape = (
        jax.ShapeDtypeStruct((b, t, CONV_DIM), BF16),
        jax.ShapeDtypeStruct((b, t, ATTN_DIM), BF16),
        jax.ShapeDtypeStruct((b, t, kvd), F32),
        jax.ShapeDtypeStruct((b, t, kvd), F32),
        jax.ShapeDtypeStruct((b, t, 2 * kvd), BF16),
        jax.ShapeDtypeStruct((b, t, kvd), BF16),
        jax.ShapeDtypeStruct((b, t, N_IDX_HEADS * LANES), BF16),
        jax.ShapeDtypeStruct((b, t, LANES), F32),
        jax.ShapeDtypeStruct((b, t, IDX_DIM), F32),
        jax.ShapeDtypeStruct((b, t, LANES), BF16),
        jax.ShapeDtypeStruct((b, CONV_W - 1, CONV_DIM), F32),
    )
    out_specs = (tile(CONV_DIM), tile(ATTN_DIM), tile(kvd), tile(kvd), tile(2 * kvd), tile(kvd),
                 tile(N_IDX_HEADS * LANES), tile(LANES), tile(IDX_DIM), tile(LANES),
                 pl.BlockSpec((1, CONV_W - 1, CONV_DIM), lambda i, j: (i, 0, 0)))
    return pl.pallas_call(
        _inproj_kernel, grid=grid, in_specs=in_specs, out_specs=out_specs, out_shape=out_shape,
        scratch_shapes=[pltpu.VMEM((8, CONV_DIM), F32)],
        compiler_params=pltpu.CompilerParams(dimension_semantics=("arbitrary", "arbitrary"),
                                             vmem_limit_bytes=V7X_VMEM_LIMIT),
        name="inproj",
    )(x, prev, *consts)


def _slabs(n):
    return [slice(j * LANES, (j + 1) * LANES) for j in range(n // LANES)]


def _tree(op, xs):
    while len(xs) > 1:
        xs = [op(xs[i], xs[i + 1]) for i in range(0, len(xs) - 1, 2)] + ([xs[-1]] if len(xs) % 2 else [])
    return xs[0]


def _attn_kernel(q_ref, iqcat_ref, iw_ref, k2_ref, v_ref, ikcat_ref, tri_ref, o_ref,
                 keyst_ref, keys_ref, lga_ref, lgb_ref, lhs_ref, wb_ref, qg_ref, sa_ref, sb_ref,
                 biasa_ref, biasb_ref, mxa_ref, mxb_ref, p_ref, m_ref, l_ref, acc_ref,
                 *, tq, kc, n_keys, q_pos0, topk):
    qi = pl.program_id(1)
    q0 = q_pos0 + qi * tq
    shift = CHUNK.bit_length() - 1

    def key_limit(pos):
        return jnp.minimum(((pos >> shift) + 1) << shift, n_keys)

    lane = lax.broadcasted_iota(I32, (tq, LANES), 1)
    lim = key_limit(q0 + lax.broadcasted_iota(I32, (tq, LANES), 0))
    qlane = lax.broadcasted_iota(I32, (1, LANES), 1)
    lim_t = jnp.where(qlane < tq, key_limit(q0 + qlane), 0)
    n_chunks = (key_limit(q0 + tq - 1) + kc - 1) // kc
    kfl = jnp.minimum(lim_t, topk).astype(F32)

    iqc = iqcat_ref[0]
    for h in range(N_IDX_HEADS):
        lhs_ref[h * tq:(h + 1) * tq, :] = iqc[:, h * LANES:(h + 1) * LANES]
    iw = iw_ref[0]
    for h in range(N_IDX_HEADS):
        col = jnp.sum(jnp.where(lane == IW_LANE + h, iw, 0.0), axis=1, keepdims=True)
        wb_ref[h] = jnp.broadcast_to(col, (tq, LANES))
    pad_rows = jnp.full((LANES - tq, LANES), INT_MIN, I32) if tq < LANES else None

    last_chunk = keys_ref.shape[0] - 1

    def logits(c, lg_ref):
        off = pl.multiple_of(jnp.minimum(c, last_chunk) * kc, kc)
        lg_ref[...] = lax.dot_general(lhs_ref[...], ikcat_ref[0, pl.ds(off, kc), :],
                                      (((1,), (1,)), ((), ())), preferred_element_type=F32)

    def keys_from(c, lg_ref, kmax):
        for j, sl in enumerate(_slabs(kc)):
            sc = None
            for h in range(N_IDX_HEADS):
                term = jnp.maximum(lg_ref[h * tq:(h + 1) * tq, sl], 0.0) * wb_ref[h]
                sc = term if sc is None else sc + term
            sc = sc + 0.0
            bits = lax.bitcast_convert_type(sc, I32)
            key = bits ^ ((bits >> 31) & INT_MAX)
            key = jnp.where(c * kc + j * LANES + lane < lim, key, INT_MIN)
            keys_ref[c, :, sl] = key
            if pad_rows is not None:
                key = jnp.concatenate([key, pad_rows], axis=0)
            kt = key.T
            keyst_ref[c, sl, :] = kt
            kmax = jnp.maximum(kmax, _tree(jnp.maximum, [kt[8 * i:8 * i + 8, :]
                                                         for i in range(LANES // 8)]))
        return kmax

    logits(0, lga_ref)

    def score_pair(i, kmax):
        logits(2 * i + 1, lgb_ref)
        kmax = keys_from(2 * i, lga_ref, kmax)
        logits(2 * i + 2, lga_ref)
        return keys_from(2 * i + 1, lgb_ref, kmax)

    odd_chunks = n_chunks % 2 == 1
    kmax = lax.fori_loop(0, n_chunks // 2, score_pair, jnp.full((8, LANES), INT_MIN, I32))
    kmax = lax.cond(odd_chunks, lambda k: keys_from(n_chunks - 1, lga_ref, k), lambda k: k, kmax)
    kmax = jnp.max(kmax, axis=0, keepdims=True)

    tree_sum = functools.partial(_tree, jnp.add)

    def count_ge(thr):
        def body(c, acc):
            hit = jnp.where(keyst_ref[c] >= thr, 1.0, 0.0)
            return acc + tree_sum([hit[8 * i:8 * i + 8, :] for i in range(kc // 8)])

        part = lax.fori_loop(0, n_chunks, body, jnp.zeros((8, LANES), F32))
        return jnp.sum(part, axis=0, keepdims=True)

    lo0 = jnp.full((1, LANES), INT_MIN + 1, I32)

    def next_probe(lo, hi):
        mid = (lo & hi) + ((lo ^ hi) >> 1)
        floor = lo == lo0
        return jnp.where(jnp.logical_and(floor, hi > 1), 1,
                         jnp.where(jnp.logical_and(floor, hi == 1), 0, mid))

    def bisect_pass(lo, hi, clo, chi, mid):
        cnt = count_ge(mid)
        ge = cnt >= kfl
        exact = cnt == kfl
        lo_n = jnp.where(ge, mid, lo)
        hi_n = jnp.where(exact, mid + 1, jnp.where(ge, hi, mid))
        return lo_n, hi_n, jnp.where(ge, cnt, clo), jnp.where(ge, chi, cnt)

    def max_below(bound):
        def body(c, acc):
            kch = keyst_ref[c]
            cand = jnp.where(kch < bound, kch, INT_MIN)
            return jnp.maximum(acc, _tree(jnp.maximum, [cand[8 * i:8 * i + 8, :] for i in range(kc // 8)]))

        part = lax.fori_loop(0, n_chunks, body, jnp.full((8, LANES), INT_MIN, I32))
        return jnp.max(part, axis=0, keepdims=True)

    def bisect_round(state):
        for _ in range(BISECT_ROUND):
            state = bisect_pass(*state, next_probe(state[0], state[1]))
        return state

    def extract_step(state):
        lo, hi, clo, chi = state
        top = max_below(hi)
        cnt = count_ge(top)
        open_ = hi != lo + 1
        done = jnp.logical_and(open_, cnt >= kfl)
        down = jnp.logical_and(open_, cnt < kfl)
        return (jnp.where(done, top, lo), jnp.where(done, top + 1, jnp.where(down, top, hi)),
                jnp.where(done, cnt, clo), jnp.where(down, cnt, chi))

    def open_need(lo, hi, chi):
        need = jnp.where(hi != lo + 1, kfl - chi, 0.0)
        return jnp.max(need, axis=1, keepdims=True)[0, 0]

    def bisect_cond(carry):
        it, need = carry[0], carry[1]
        return jnp.logical_and(need > 0.0, it < BISECT_MAX_ROUNDS)

    def bisect_body(carry):
        it, need = carry[0], carry[1]
        state = lax.cond(need <= EXTRACT_BELOW, extract_step, bisect_round, tuple(carry[2:]))
        return (it + 1, open_need(state[0], state[1], state[3])) + tuple(state)

    hi = jnp.maximum(jnp.minimum(kmax, INT_MAX - 1) + 1, lo0 + 1)
    first = jnp.where(kmax > INT_MIN + 2 + FIRST_PROBE_GAP, kmax - FIRST_PROBE_GAP, next_probe(lo0, hi))
    state = bisect_pass(lo0, hi, lim_t.astype(F32), jnp.zeros((1, LANES), F32), first)
    _, _, thr, _, clo, chi = lax.while_loop(
        bisect_cond, bisect_body, (jnp.int32(0), open_need(state[0], state[1], state[3])) + tuple(state))

    extra = clo - kfl
    any_tie = jnp.max(extra, axis=1, keepdims=True)[0, 0]

    @pl.when(any_tie > 0.0)
    def _():
        need = jnp.where(extra > 0.0, kfl - chi, jnp.inf)

        def demote(c, seen):
            kch = keyst_ref[c]
            tied = kch == thr
            rank = seen + jnp.dot(tri_ref[...], jnp.where(tied, 1.0, 0.0).astype(BF16),
                                  preferred_element_type=F32)
            kch = jnp.where(jnp.logical_and(tied, rank > need), kch - 1, kch)
            keyst_ref[c] = kch
            for sl in _slabs(kc):
                keys_ref[c, :, sl] = kch[sl, :].T[:tq, :]
            return rank[kc - 1:kc, :]

        lax.fori_loop(0, n_chunks, demote, jnp.zeros((1, LANES), F32))

    thr_rows = jnp.broadcast_to(thr, (LANES, LANES)).T[:tq, :]
    qv = q_ref[0].astype(F32)
    heads_per_group = N_HEADS // N_KV
    rows = heads_per_group * tq
    for g in range(N_KV):
        for r in range(heads_per_group):
            hd = g * heads_per_group + r
            slab = qv[:, (hd // 2) * LANES:(hd // 2 + 1) * LANES]
            keep = (lane < HEAD_DIM) if hd % 2 == 0 else (lane >= HEAD_DIM)
            qg_ref[g, r * tq:(r + 1) * tq, :] = jnp.where(keep, slab, 0.0).astype(BF16)
    m_ref[...] = jnp.full(m_ref.shape, NEG, F32)
    l_ref[...] = jnp.zeros(l_ref.shape, F32)
    acc_ref[...] = jnp.zeros(acc_ref.shape, F32)

    def qk_scores(c, s_ref, bias_ref, mx_ref):
        c = jnp.minimum(c, last_chunk)
        off = pl.multiple_of(c * kc, kc)
        for sl in _slabs(kc):
            bias_ref[:, sl] = jnp.where(keys_ref[c, :, sl] >= thr_rows, 0.0, NEG)
        for g in range(N_KV):
            kch = k2_ref[0, pl.ds(off, kc), g * LANES:(g + 1) * LANES]
            s = lax.dot_general(qg_ref[g], kch, (((1,), (1,)), ((), ())),
                                preferred_element_type=F32)
            for r in range(heads_per_group):
                rs = slice(r * tq, (r + 1) * tq)
                masked = [s[rs, sl] + bias_ref[:, sl] for sl in _slabs(kc)]
                for sl, piece in zip(_slabs(kc), masked):
                    s_ref[g, rs, sl] = piece
                part = _tree(jnp.maximum, masked)
                mx_ref[g, rs, :] = jnp.broadcast_to(jnp.max(part, axis=1, keepdims=True), (tq, LANES))

    def softmax_pv(c, s_ref, bias_ref, mx_ref):
        off = pl.multiple_of(c * kc, kc)
        vch = v_ref[0, pl.ds(off, kc), :]
        for g in range(N_KV):
            m_prev = m_ref[g]
            m_new = jnp.maximum(m_prev, mx_ref[g])
            alpha = jnp.exp2(m_prev - m_new)
            for r in range(heads_per_group):
                rs = slice(r * tq, (r + 1) * tq)
                ps = [jnp.exp2(s_ref[g, rs, sl] - m_new[rs]) for sl in _slabs(kc)]
                p_ref[g, rs, :] = jnp.concatenate(ps, axis=1).astype(BF16)
                l_ref[g, rs, :] = alpha[rs] * l_ref[g, rs, :] + tree_sum(ps)
            acc_ref[g] = alpha * acc_ref[g] + jnp.dot(p_ref[g], vch, preferred_element_type=F32)
            m_ref[g] = m_new

    stage_a = (sa_ref, biasa_ref, mxa_ref)
    stage_b = (sb_ref, biasb_ref, mxb_ref)
    qk_scores(0, *stage_a)

    def attend_pair(i, carry):
        qk_scores(2 * i + 1, *stage_b)
        softmax_pv(2 * i, *stage_a)
        qk_scores(2 * i + 2, *stage_a)
        softmax_pv(2 * i + 1, *stage_b)
        return carry

    lax.fori_loop(0, n_chunks // 2, attend_pair, 0)

    @pl.when(odd_chunks)
    def _():
        softmax_pv(n_chunks - 1, *stage_a)

    for g in range(N_KV):
        out = acc_ref[g] / jnp.sum(l_ref[g], axis=1, keepdims=True)
        for jj in range(heads_per_group // 2):
            a = out[(2 * jj) * tq:(2 * jj + 1) * tq, :]
            bb = out[(2 * jj + 1) * tq:(2 * jj + 2) * tq, :]
            if g % 2 == 0:
                slab = jnp.where(lane < HEAD_DIM, a, pltpu.roll(bb, HEAD_DIM, axis=1))
            else:
                slab = jnp.where(lane < HEAD_DIM, pltpu.roll(a, HEAD_DIM, axis=1), bb)
            so = (g * heads_per_group // 2 + jj) * LANES
            o_ref[0, :, so:so + LANES] = slab.astype(BF16)


def _attention(q, iqcat, iw, k2, vb, ikcat, *, tq, n_keys, q_pos0):
    b, t, _ = q.shape
    lpad = k2.shape[1]
    kc = KEY_CHUNK
    assert lpad % kc == 0 and t % tq == 0 and tq <= LANES and CHUNK & (CHUNK - 1) == 0
    topk = min(TOPK_MAX, n_keys // 4)
    kern = functools.partial(_attn_kernel, tq=tq, kc=kc, n_keys=n_keys, q_pos0=q_pos0, topk=topk)
    qtile = lambda n: pl.BlockSpec((1, tq, n), lambda i, j: (i, j, 0))
    ktile = lambda n: pl.BlockSpec((1, lpad, n), lambda i, j: (i, 0, 0))
    rows = (N_HEADS // N_KV) * tq
    return pl.pallas_call(
        kern, grid=(b, t // tq),
        in_specs=[qtile(ATTN_DIM), qtile(N_IDX_HEADS * LANES), qtile(LANES),
                  ktile(2 * N_KV * HEAD_DIM), ktile(N_KV * HEAD_DIM), ktile(LANES),
                  pl.BlockSpec((kc, kc), lambda i, j: (0, 0))],
        out_specs=qtile(ATTN_DIM),
        out_shape=jax.ShapeDtypeStruct((b, t, ATTN_DIM), BF16),
        scratch_shapes=[
            pltpu.VMEM((lpad // kc, kc, LANES), I32),
            pltpu.VMEM((lpad // kc, tq, kc), I32),
            pltpu.VMEM((N_IDX_HEADS * tq, kc), F32),
            pltpu.VMEM((N_IDX_HEADS * tq, kc), F32),
            pltpu.VMEM((N_IDX_HEADS * tq, LANES), BF16),
            pltpu.VMEM((N_IDX_HEADS, tq, LANES), F32),
            pltpu.VMEM((N_KV, rows, LANES), BF16),
            pltpu.VMEM((N_KV, rows, kc), F32),
            pltpu.VMEM((N_KV, rows, kc), F32),
            pltpu.VMEM((tq, kc), F32),
            pltpu.VMEM((tq, kc), F32),
            pltpu.VMEM((N_KV, rows, LANES), F32),
            pltpu.VMEM((N_KV, rows, LANES), F32),
            pltpu.VMEM((N_KV, rows, kc), BF16),
            pltpu.VMEM((N_KV, rows, LANES), F32),
            pltpu.VMEM((N_KV, rows, LANES), F32),
            pltpu.VMEM((N_KV, rows, LANES), F32),
        ],
        compiler_params=pltpu.CompilerParams(dimension_semantics=("arbitrary", "arbitrary"),
                                             vmem_limit_bytes=V7X_VMEM_LIMIT),
        name="attention",
    )(q, iqcat, iw, k2, vb, ikcat, jnp.asarray(np.tril(np.ones((kc, kc), np.float32)), BF16))


def _mix_in(x_ref, yc_ref, at_ref, woa_ref, wob_ref, g_ref):
    x1 = (x_ref[...] + jnp.dot(yc_ref[...], woa_ref[...], preferred_element_type=F32)
          + jnp.dot(at_ref[...], wob_ref[...], preferred_element_type=F32))
    return x1, _rmsnorm(x1, g_ref[...])


def _swiglu(hb, wg, wu, wd):
    a = jnp.dot(hb, wg, preferred_element_type=F32)
    b = jnp.dot(hb, wu, preferred_element_type=F32)
    act = (a * (1.0 / (1.0 + jnp.exp(-a))) * b).astype(BF16)
    return jnp.dot(act, wd, preferred_element_type=F32)


def _ffn_dense_kernel(x_ref, yc_ref, at_ref, woa_ref, wob_ref, g_ref, wg_ref, wu_ref, wd_ref, o_ref):
    x1, hn = _mix_in(x_ref, yc_ref, at_ref, woa_ref, wob_ref, g_ref)
    o_ref[...] = x1 + _swiglu(hn.astype(BF16), wg_ref[...], wu_ref[...], wd_ref[...])


def _ffn_dense(x, yc, at, lw, tm):
    m, d = x.shape
    row = lambda n: pl.BlockSpec((tm, n), lambda i: (i, 0))
    consts = [lw["wo_a"], lw["wo_b"], lw["ffn_g"], lw["wg"], lw["wu"], lw["wd"]]
    return pl.pallas_call(
        _ffn_dense_kernel, grid=(m // tm,),
        in_specs=[row(d), row(CONV_DIM), row(ATTN_DIM)]
        + [pl.BlockSpec(c.shape, lambda i: (0, 0), pipeline_mode=pl.Buffered(1)) for c in consts],
        out_specs=row(d), out_shape=jax.ShapeDtypeStruct((m, d), F32),
        compiler_params=pltpu.CompilerParams(dimension_semantics=("arbitrary",),
                                             vmem_limit_bytes=V7X_VMEM_LIMIT),
        name="ffn_dense",
    )(x, yc, at, *consts)


def _ffn_moe_kernel(x_ref, yc_ref, at_ref, woa_ref, wob_ref, g_ref, rhi_ref, rlo_ref, tril_ref,
                    eg_ref, eu_ref, ed_ref, o_ref, h_ref, gate_ref, rank_ref, rankt_ref, cnt_ref, acc_ref,
                    *, cap):
    e = pl.program_id(1)
    tm = x_ref.shape[0]
    lane = lax.broadcasted_iota(I32, (tm, LANES), 1)

    @pl.when(e == 0)
    def _():
        x1, hn = _mix_in(x_ref, yc_ref, at_ref, woa_ref, wob_ref, g_ref)
        hi, lo = _split_hi_lo(hn)
        h_ref[...] = hi
        rhi = rhi_ref[...]
        logits = (jnp.dot(hi, rhi, preferred_element_type=F32)
                  + jnp.dot(lo, rhi, preferred_element_type=F32)
                  + jnp.dot(hi, rlo_ref[...], preferred_element_type=F32))
        logits = jnp.where(lane < N_EXPERTS, logits, -jnp.inf)
        ex = jnp.exp(logits - jnp.max(logits, axis=1, keepdims=True))
        probs = ex / jnp.sum(ex, axis=1, keepdims=True)
        p1 = jnp.max(probs, axis=1, keepdims=True)
        i1 = jnp.min(jnp.where(probs == p1, lane, LANES), axis=1, keepdims=True)
        rest = jnp.where(lane == i1, -1.0, probs)
        p2 = jnp.max(rest, axis=1, keepdims=True)
        i2 = jnp.min(jnp.where(rest == p2, lane, LANES), axis=1, keepdims=True)
        den = p1 + p2
        gate_ref[...] = jnp.where(lane == i1, p1 / den, jnp.where(lane == i2, p2 / den, 0.0))
        routed = jnp.where(jnp.logical_or(lane == i1, lane == i2), 1.0, 0.0)
        before = jnp.dot(tril_ref[...], routed.astype(BF16), preferred_element_type=F32)
        rank = jnp.where(routed > 0.0, before, -1.0)
        rank_ref[...] = rank
        for r0 in range(0, tm, LANES):
            nr = min(LANES, tm - r0)
            blk = rank[r0:r0 + nr, :]
            if nr < LANES:
                blk = jnp.concatenate([blk, jnp.full((LANES - nr, LANES), -1.0, F32)], axis=0)
            rt = blk.T
            for ex_id in range(N_EXPERTS):
                rankt_ref[ex_id, :, r0:r0 + nr] = rt[ex_id:ex_id + 1, :nr]
        cnt_ref[...] = jnp.broadcast_to(jnp.sum(routed, axis=0, keepdims=True), cnt_ref.shape)
        acc_ref[...] = x1

    mine = lane == e
    rank_col = jnp.sum(jnp.where(mine, rank_ref[...], 0.0), axis=1, keepdims=True)
    gate_col = jnp.sum(jnp.where(mine, gate_ref[...], 0.0), axis=1, keepdims=True)
    rank_row = rankt_ref[e]
    n_tok = jnp.sum(jnp.where(mine[:8], cnt_ref[...], 0.0), axis=1, keepdims=True)[0, 0]
    n_sub = (n_tok.astype(I32) + cap - 1) // cap
    kpad = -(-cap // LANES) * LANES
    slot_r = lax.broadcasted_iota(I32, (cap, tm), 0).astype(F32)
    slot_c = lax.broadcasted_iota(I32, (tm, kpad), 1).astype(F32)

    def sub_block(sb, carry):
        base = (sb * cap).astype(F32)
        gather = jnp.where(rank_row - base == slot_r, 1.0, 0.0).astype(BF16)
        hs = jnp.dot(gather, h_ref[...], preferred_element_type=F32).astype(BF16)
        y = _swiglu(hs, eg_ref[0], eu_ref[0], ed_ref[0]).astype(BF16)
        if kpad > cap:
            y = jnp.concatenate([y, jnp.zeros((kpad - cap, y.shape[1]), BF16)], axis=0)
        scatter = jnp.where(rank_col - base == slot_c, 1.0, 0.0).astype(BF16)
        acc_ref[...] += gate_col * jnp.dot(scatter, y, preferred_element_type=F32)
        return carry

    lax.fori_loop(0, n_sub, sub_block, 0)

    @pl.when(e == pl.num_programs(1) - 1)
    def _():
        o_ref[...] = acc_ref[...]


def _ffn_moe(x, yc, at, lw, tm):
    m, d = x.shape
    ne, _, fe = lw["eg"].shape
    cap = min(tm, -(-(9 * tm // 32) // 16) * 16)
    row = lambda n: pl.BlockSpec((tm, n), lambda i, e: (i, 0))
    tril = jnp.asarray(np.tril(np.ones((tm, tm), np.float32), -1), BF16)
    consts = [lw["wo_a"], lw["wo_b"], lw["ffn_g"], lw["r_hi"], lw["r_lo"], tril]
    return pl.pallas_call(
        functools.partial(_ffn_moe_kernel, cap=cap), grid=(m // tm, ne),
        in_specs=[row(d), row(CONV_DIM), row(ATTN_DIM)]
        + [pl.BlockSpec(c.shape, lambda i, e: (0, 0), pipeline_mode=pl.Buffered(1)) for c in consts]
        + [pl.BlockSpec((1, d, fe), lambda i, e: (e, 0, 0)),
           pl.BlockSpec((1, d, fe), lambda i, e: (e, 0, 0)),
           pl.BlockSpec((1, fe, d), lambda i, e: (e, 0, 0))],
        out_specs=row(d), out_shape=jax.ShapeDtypeStruct((m, d), F32),
        scratch_shapes=[pltpu.VMEM((tm, d), BF16),
                        pltpu.VMEM((tm, LANES), F32),
                        pltpu.VMEM((tm, LANES), F32),
                        pltpu.VMEM((ne, 1, tm), F32),
                        pltpu.VMEM((8, LANES), F32),
                        pltpu.VMEM((tm, d), F32)],
        compiler_params=pltpu.CompilerParams(dimension_semantics=("arbitrary", "arbitrary"),
                                             vmem_limit_bytes=V7X_VMEM_LIMIT),
        name="ffn_moe",
    )(x, yc, at, *consts, lw["eg"], lw["eu"], lw["ed"])


def _block_diag_mean(n, group, valid=None):
    m = np.zeros((n, n), np.float32)
    for s in range(0, n if valid is None else valid, group):
        m[s:s + group, s:s + group] = 1.0 / group
    return jnp.asarray(m, BF16)


def _placement_matrices():
    s1 = np.zeros((N_IDX_HEADS * IDX_DIM, N_IDX_HEADS * LANES), np.float32)
    s2 = np.zeros_like(s1)
    t1 = np.zeros((LANES, LANES), np.float32)
    t2 = np.zeros_like(t1)
    j = np.arange(IDX_DIM)
    for h in range(N_IDX_HEADS):
        s1[h * IDX_DIM + j, h * LANES + j] = 1.0
        s2[h * IDX_DIM + j, h * LANES + IDX_DIM + j] = 1.0
        s1[h * IDX_DIM + j, h * LANES + 2 * IDX_DIM + j] = 1.0
    t1[j, j] = 1.0
    t1[j, IDX_DIM + j] = 1.0
    t2[j, 2 * IDX_DIM + j] = 1.0
    return tuple(jnp.asarray(a, BF16) for a in (s1, s2, t1, t2))


def _layer_weights(l, p):
    d = p["w_in"].shape[1]
    s1, s2, t1, t2 = _placement_matrices()
    lw = {
        "attn_g": p["attn_norm_g"][l][None],
        "w_in": jnp.pad(p["w_in"][l], ((0, 0), (0, IN_COLS_PAD - IN_COLS))).astype(BF16),
        "conv_w": p["conv_w"][l],
        "gq": jnp.tile(p["q_norm_g"][l], N_HEADS)[None],
        "gk": jnp.tile(p["k_norm_g"][l], N_KV)[None],
        "giq": jnp.tile(p["iq_norm_g"][l], N_IDX_HEADS)[None],
        "gik": jnp.pad(p["ik_norm_g"][l], (0, LANES - IDX_DIM))[None],
        "mq": _block_diag_mean(ATTN_DIM, HEAD_DIM),
        "mk": _block_diag_mean(N_KV * HEAD_DIM, HEAD_DIM),
        "miq": _block_diag_mean(N_IDX_HEADS * IDX_DIM, IDX_DIM),
        "mik": _block_diag_mean(LANES, IDX_DIM, valid=IDX_DIM),
        "s1": s1, "s2": s2, "t1": t1, "t2": t2,
        "wo_a": p["w_out"][l][:CONV_DIM].astype(BF16),
        "wo_b": p["w_out"][l][CONV_DIM:].astype(BF16),
        "ffn_g": p["ffn_norm_g"][l][None],
    }
    j = l // 2
    if l % 2 == 0:
        lw.update(wg=p["ffn_w_gate"][j].astype(BF16), wu=p["ffn_w_up"][j].astype(BF16),
                  wd=p["ffn_w_down"][j].astype(BF16))
    else:
        r = jnp.pad(p["router_w"][j], ((0, 0), (0, LANES - N_EXPERTS)))
        r_hi = r.astype(BF16)
        lw.update(r_hi=r_hi, r_lo=(r - r_hi.astype(F32)).astype(BF16),
                  eg=p["moe_w_gate"][j].astype(BF16), eu=p["moe_w_up"][j].astype(BF16),
                  ed=p["moe_w_down"][j].astype(BF16))
    assert d == lw["wo_a"].shape[1]
    return lw


def _pick_tile(n, pref):
    t = min(n, pref)
    while n % t:
        t //= 2
    return t


def _mixer(x, prev, lw, *, tq, past_k2=None, past_v=None, past_ik=None, q_pos0=0):
    b, t, d = x.shape
    yc, q, k, v, k2, vb, iqcat, iw, ik, ikcat, cst = _inproj(x, prev, lw, _pick_tile(t, 512))
    if past_k2 is not None:
        k2 = jnp.concatenate([past_k2, k2], axis=1)
        vb = jnp.concatenate([past_v, vb], axis=1)
        ikcat = jnp.concatenate([past_ik, ikcat], axis=1)
    n_keys = k2.shape[1]
    pad = (-n_keys) % KEY_CHUNK
    if pad:
        k2, vb, ikcat = (jnp.pad(a, ((0, 0), (0, pad), (0, 0))) for a in (k2, vb, ikcat))
    at = _attention(q, iqcat, iw, k2, vb, ikcat, tq=tq, n_keys=n_keys, q_pos0=q_pos0)
    m = b * t
    x2, yc2, at2 = x.reshape(m, d), yc.reshape(m, CONV_DIM), at.reshape(m, ATTN_DIM)
    if "wg" in lw:
        y = _ffn_dense(x2, yc2, at2, lw, _pick_tile(m, 512))
    else:
        y = _ffn_moe(x2, yc2, at2, lw, _pick_tile(m, 1024))
    return (y.reshape(b, t, d), k.reshape(b, t, N_KV, HEAD_DIM), v.reshape(b, t, N_KV, HEAD_DIM),
            ik, cst)


def kernel(x_prompt, x_sample, cache_k, cache_v, cache_ik, state_conv, attn_norm_g, w_in, conv_w,
           q_norm_g, k_norm_g, iq_norm_g, ik_norm_g, w_out, ffn_norm_g, ffn_w_gate, ffn_w_up,
           ffn_w_down, router_w, moe_w_gate, moe_w_up, moe_w_down):
    params = dict(attn_norm_g=attn_norm_g, w_in=w_in, conv_w=conv_w, q_norm_g=q_norm_g,
                  k_norm_g=k_norm_g, iq_norm_g=iq_norm_g, ik_norm_g=ik_norm_g, w_out=w_out,
                  ffn_norm_g=ffn_norm_g, ffn_w_gate=ffn_w_gate, ffn_w_up=ffn_w_up,
                  ffn_w_down=ffn_w_down, router_w=router_w, moe_w_gate=moe_w_gate,
                  moe_w_up=moe_w_up, moe_w_down=moe_w_down)
    depth = w_in.shape[0]
    xp, xs = x_prompt, x_sample
    bp = xp.shape[0]
    bs, ts, _ = xs.shape
    past = cache_k.shape[2]
    kvd = N_KV * HEAD_DIM
    outs = [[] for _ in range(8)]
    for l in range(depth):
        lw = _layer_weights(l, params)
        prev0 = jnp.zeros((bp, CONV_W - 1, CONV_DIM), F32)
        xp, kp, vp, ikp, cp = _mixer(xp, prev0, lw, tq=Q_BLOCK)

        ck = cache_k[l].reshape(bs, past, kvd)
        g0, g1 = ck[..., :HEAD_DIM], ck[..., HEAD_DIM:]
        past_k2 = jnp.concatenate([g0, g0, g1, g1], axis=-1).astype(BF16)
        past_v = cache_v[l].reshape(bs, past, kvd).astype(BF16)
        cik = cache_ik[l]
        cik_hi = cik.astype(BF16)
        cik_lo = (cik - cik_hi.astype(F32)).astype(BF16)
        past_ik = jnp.concatenate([cik_hi, cik_hi, cik_lo, jnp.zeros_like(cik_hi)], axis=-1)
        xs, ks, vs, iks, cs = _mixer(xs, state_conv[l], lw, tq=ts, past_k2=past_k2,
                                     past_v=past_v, past_ik=past_ik, q_pos0=past)
        for lst, val in zip(outs, (kp, vp, ikp, cp, ks, vs, iks, cs)):
            lst.append(val)
    kp, vp, ikp, cp, ks, vs, iks, cs = (jnp.stack(o) for o in outs)
    return (xp, xs, kp, vp, ikp, cp, ks, vs, iks, cs)
```

```python
import functools
import math

import numpy as np
import jax
import jax.numpy as jnp
from jax import lax
from jax.experimental import pallas as pl
from jax.experimental.pallas import tpu as pltpu

CHUNK = 64
CONV_DIM = 512
CONV_W = 3
N_HEADS = 8
HEAD_DIM = 64
ATTN_DIM = N_HEADS * HEAD_DIM
N_KV = 2
N_IDX_HEADS = 8
IDX_DIM = 32
TOPK_MAX = 256
Q_BLOCK = 128
N_EXPERTS = 8
EPS = 1e-6
COL_SIZES = (CONV_DIM, CONV_DIM, CONV_DIM, ATTN_DIM, N_KV * HEAD_DIM, N_KV * HEAD_DIM,
             N_IDX_HEADS * IDX_DIM, IDX_DIM, N_IDX_HEADS)
IN_COLS = sum(COL_SIZES)

LANES = 128
V7X_VMEM_LIMIT = 56 * 1024 * 1024

IN_COLS_PAD = ((IN_COLS + LANES - 1) // LANES) * LANES
TAIL_OFF = IN_COLS_PAD - LANES
IW_LANE = IDX_DIM
QSCALE = HEAD_DIM ** -0.5 * math.log2(math.e)
IWSCALE = IDX_DIM ** -0.5 * N_IDX_HEADS ** -0.5
NEG = -1e30
INT_MIN = -2 ** 31
INT_MAX = 2 ** 31 - 1
KEY_CHUNK = 512
BISECT_ROUND = 4
EXTRACT_BELOW = 2
BISECT_MAX_ROUNDS = -(-(32 + 3) // BISECT_ROUND) + EXTRACT_BELOW
FIRST_PROBE_GAP = 3 << 23

F32 = jnp.float32
BF16 = jnp.bfloat16
I32 = jnp.int32


def _split_hi_lo(x):
    hi = x.astype(BF16)
    lo = (x - hi.astype(F32)).astype(BF16)
    return hi, lo


def _group_rsqrt(x, g_ref):
    hi, lo = _split_hi_lo(x * x)
    g = g_ref[...]
    ms = jnp.dot(hi, g, preferred_element_type=F32) + jnp.dot(lo, g, preferred_element_type=F32)
    return lax.rsqrt(ms + EPS)


def _rmsnorm(x, g):
    return x * lax.rsqrt(jnp.mean(x * x, axis=-1, keepdims=True) + EPS) * g


def _inproj_kernel(x_ref, prev_ref, g_ref, w_ref, cw_ref, gq_ref, gk_ref, giq_ref, gik_ref,
                   mq_ref, mk_ref, miq_ref, mik_ref, s1_ref, s2_ref, t1_ref, t2_ref,
                   yconv_ref, q_ref, k_ref, v_ref, k2_ref, vb_ref, iqcat_ref, iw_ref, ik_ref,
                   ikcat_ref, cst_ref, carry_ref):
    t = pl.program_id(1)
    nt = pl.num_programs(1)
    x = x_ref[0]
    tm = x.shape[0]
    h = _rmsnorm(x, g_ref[...]).astype(BF16)
    p = jnp.dot(h, w_ref[...], preferred_element_type=F32)
    o = np.cumsum((0,) + COL_SIZES)
    c_gate, b_gate, u = p[:, o[0]:o[1]], p[:, o[1]:o[2]], p[:, o[2]:o[3]]
    q, k, v, iq = p[:, o[3]:o[4]], p[:, o[4]:o[5]], p[:, o[5]:o[6]], p[:, o[6]:o[7]]
    tail = p[:, TAIL_OFF:]

    ci = c_gate * u

    @pl.when(t == 0)
    def _():
        carry_ref[8 - (CONV_W - 1):, :] = prev_ref[0]

    row = lax.broadcasted_iota(I32, ci.shape, 0)
    c7 = carry_ref[7:8, :]
    c6 = carry_ref[6:7, :]
    s1 = jnp.where(row == 0, c7, pltpu.roll(ci, 1, axis=0))
    s2 = jnp.where(row == 0, c6, jnp.where(row == 1, c7, pltpu.roll(ci, 2, axis=0)))
    cw = cw_ref[...]
    conv = cw[0:1] * s2 + cw[1:2] * s1 + cw[2:3] * ci
    yconv_ref[0] = (b_gate * conv).astype(BF16)
    carry_ref[...] = ci[tm - 8:, :]

    @pl.when(t == nt - 1)
    def _():
        cst_ref[0] = ci[tm - (CONV_W - 1):, :]

    qn = q * _group_rsqrt(q, mq_ref) * gq_ref[...]
    q_ref[0] = (qn * QSCALE).astype(BF16)
    kn = k * _group_rsqrt(k, mk_ref) * gk_ref[...]
    k_ref[0] = kn
    v_ref[0] = v
    lane = lax.broadcasted_iota(I32, kn.shape, 1)
    kr = pltpu.roll(kn, HEAD_DIM, axis=1)
    k2_ref[0, :, 0:LANES] = jnp.where(lane < HEAD_DIM, kn, kr).astype(BF16)
    k2_ref[0, :, LANES:2 * LANES] = jnp.where(lane < HEAD_DIM, kr, kn).astype(BF16)
    vb_ref[0] = v.astype(BF16)

    iqn = iq * _group_rsqrt(iq, miq_ref) * giq_ref[...]
    ihi, ilo = _split_hi_lo(iqn)
    iqcat_ref[0] = (jnp.dot(ihi, s1_ref[...], preferred_element_type=F32)
                    + jnp.dot(ilo, s2_ref[...], preferred_element_type=F32)).astype(BF16)
    tn = tail * _group_rsqrt(tail, mik_ref) * gik_ref[...]
    ik_ref[0] = tn[:, 0:IDX_DIM]
    thi, tlo = _split_hi_lo(tn)
    ikcat_ref[0] = (jnp.dot(thi, t1_ref[...], preferred_element_type=F32)
                    + jnp.dot(tlo, t2_ref[...], preferred_element_type=F32)).astype(BF16)
    iw_ref[0] = tail * IWSCALE


def _const_spec(shape):
    nd = len(shape)
    return pl.BlockSpec(shape, lambda *_: (0,) * nd)


def _inproj(x, prev, lw, tm):
    b, t, d = x.shape
    grid = (b, t // tm)
    tile = lambda n: pl.BlockSpec((1, tm, n), lambda i, j: (i, j, 0))
    consts = [lw["attn_g"], lw["w_in"], lw["conv_w"], lw["gq"], lw["gk"], lw["giq"], lw["gik"],
              lw["mq"], lw["mk"], lw["miq"], lw["mik"], lw["s1"], lw["s2"], lw["t1"], lw["t2"]]
    in_specs = ([tile(d), pl.BlockSpec((1, CONV_W - 1, CONV_DIM), lambda i, j: (i, 0, 0))]
                + [_const_spec(c.shape) for c in consts])
    kvd = N_KV * HEAD_DIM
    out_shape = (
        jax.ShapeDtypeStruct((b, t, CONV_DIM), BF16),
        jax.ShapeDtypeStruct((b, t, ATTN_DIM), BF16),
        jax.ShapeDtypeStruct((b, t, kvd), F32),
        jax.ShapeDtypeStruct((b, t, kvd), F32),
        jax.ShapeDtypeStruct((b, t, 2 * kvd), BF16),
        jax.ShapeDtypeStruct((b, t, kvd), BF16),
        jax.ShapeDtypeStruct((b, t, N_IDX_HEADS * LANES), BF16),
        jax.ShapeDtypeStruct((b, t, LANES), F32),
        jax.ShapeDtypeStruct((b, t, IDX_DIM), F32),
        jax.ShapeDtypeStruct((b, t, LANES), BF16),
        jax.ShapeDtypeStruct((b, CONV_W - 1, CONV_DIM), F32),
    )
    out_specs = (tile(CONV_DIM), tile(ATTN_DIM), tile(kvd), tile(kvd), tile(2 * kvd), tile(kvd),
                 tile(N_IDX_HEADS * LANES), tile(LANES), tile(IDX_DIM), tile(LANES),
                 pl.BlockSpec((1, CONV_W - 1, CONV_DIM), lambda i, j: (i, 0, 0)))
    return pl.pallas_call(
        _inproj_kernel, grid=grid, in_specs=in_specs, out_specs=out_specs, out_shape=out_shape,
        scratch_shapes=[pltpu.VMEM((8, CONV_DIM), F32)],
        compiler_params=pltpu.CompilerParams(dimension_semantics=("arbitrary", "arbitrary"),
                                             vmem_limit_bytes=V7X_VMEM_LIMIT),
        name="inproj",
    )(x, prev, *consts)


def _slabs(n):
    return [slice(j * LANES, (j + 1) * LANES) for j in range(n // LANES)]


def _tree(op, xs):
    while len(xs) > 1:
        xs = [op(xs[i], xs[i + 1]) for i in range(0, len(xs) - 1, 2)] + ([xs[-1]] if len(xs) % 2 else [])
    return xs[0]


def _attn_kernel(q_ref, iqcat_ref, iw_ref, k2_ref, v_ref, ikcat_ref, tri_ref, o_ref,
                 keyst_ref, keys_ref, lga_ref, lgb_ref, lhs_ref, wb_ref, qg_ref, sa_ref, sb_ref,
                 bias_ref, mxa_ref, mxb_ref, p_ref, m_ref, l_ref, acc_ref,
                 *, tq, kc, n_keys, q_pos0, topk):
    qi = pl.program_id(1)
    q0 = q_pos0 + qi * tq
    shift = CHUNK.bit_length() - 1

    def key_limit(pos):
        return jnp.minimum(((pos >> shift) + 1) << shift, n_keys)

    lane = lax.broadcasted_iota(I32, (tq, LANES), 1)
    lim = key_limit(q0 + lax.broadcasted_iota(I32, (tq, LANES), 0))
    qlane = lax.broadcasted_iota(I32, (1, LANES), 1)
    lim_t = jnp.where(qlane < tq, key_limit(q0 + qlane), 0)
    n_chunks = (key_limit(q0 + tq - 1) + kc - 1) // kc
    kfl = jnp.minimum(lim_t, topk).astype(F32)

    iqc = iqcat_ref[0]
    for h in range(N_IDX_HEADS):
        lhs_ref[h * tq:(h + 1) * tq, :] = iqc[:, h * LANES:(h + 1) * LANES]
    iw = iw_ref[0]
    for h in range(N_IDX_HEADS):
        col = jnp.sum(jnp.where(lane == IW_LANE + h, iw, 0.0), axis=1, keepdims=True)
        wb_ref[h] = jnp.broadcast_to(col, (tq, LANES))
    pad_rows = jnp.full((LANES - tq, LANES), INT_MIN, I32) if tq < LANES else None

    last_chunk = keys_ref.shape[0] - 1

    def logits(c, lg_ref):
        off = pl.multiple_of(jnp.minimum(c, last_chunk) * kc, kc)
        lg_ref[...] = lax.dot_general(lhs_ref[...], ikcat_ref[0, pl.ds(off, kc), :],
                                      (((1,), (1,)), ((), ())), preferred_element_type=F32)

    def keys_from(c, lg_ref, kmax):
        for j, sl in enumerate(_slabs(kc)):
            sc = None
            for h in range(N_IDX_HEADS):
                term = jnp.maximum(lg_ref[h * tq:(h + 1) * tq, sl], 0.0) * wb_ref[h]
                sc = term if sc is None else sc + term
            sc = sc + 0.0
            bits = lax.bitcast_convert_type(sc, I32)
            key = bits ^ ((bits >> 31) & INT_MAX)
            key = jnp.where(c * kc + j * LANES + lane < lim, key, INT_MIN)
            keys_ref[c, :, sl] = key
            if pad_rows is not None:
                key = jnp.concatenate([key, pad_rows], axis=0)
            kt = key.T
            keyst_ref[c, sl, :] = kt
            kmax = jnp.maximum(kmax, _tree(jnp.maximum, [kt[8 * i:8 * i + 8, :]
                                                         for i in range(LANES // 8)]))
        return kmax

    logits(0, lga_ref)

    def score_pair(i, kmax):
        logits(2 * i + 1, lgb_ref)
        kmax = keys_from(2 * i, lga_ref, kmax)
        logits(2 * i + 2, lga_ref)
        return keys_from(2 * i + 1, lgb_ref, kmax)

    odd_chunks = n_chunks % 2 == 1
    kmax = lax.fori_loop(0, n_chunks // 2, score_pair, jnp.full((8, LANES), INT_MIN, I32))
    kmax = lax.cond(odd_chunks, lambda k: keys_from(n_chunks - 1, lga_ref, k), lambda k: k, kmax)
    kmax = jnp.max(kmax, axis=0, keepdims=True)

    tree_sum = functools.partial(_tree, jnp.add)

    def count_ge(thr):
        def body(c, acc):
            hit = jnp.where(keyst_ref[c] >= thr, 1.0, 0.0)
            return acc + tree_sum([hit[8 * i:8 * i + 8, :] for i in range(kc // 8)])

        part = lax.fori_loop(0, n_chunks, body, jnp.zeros((8, LANES), F32))
        return jnp.sum(part, axis=0, keepdims=True)

    lo0 = jnp.full((1, LANES), INT_MIN + 1, I32)

    def next_probe(lo, hi):
        mid = (lo & hi) + ((lo ^ hi) >> 1)
        floor = lo == lo0
        return jnp.where(jnp.logical_and(floor, hi > 1), 1,
                         jnp.where(jnp.logical_and(floor, hi == 1), 0, mid))

    def bisect_pass(lo, hi, clo, chi, mid):
        cnt = count_ge(mid)
        ge = cnt >= kfl
        exact = cnt == kfl
        lo_n = jnp.where(ge, mid, lo)
        hi_n = jnp.where(exact, mid + 1, jnp.where(ge, hi, mid))
        return lo_n, hi_n, jnp.where(ge, cnt, clo), jnp.where(ge, chi, cnt)

    def max_below(bound):
        def body(c, acc):
            kch = keyst_ref[c]
            cand = jnp.where(kch < bound, kch, INT_MIN)
            return jnp.maximum(acc, _tree(jnp.maximum, [cand[8 * i:8 * i + 8, :] for i in range(kc // 8)]))

        part = lax.fori_loop(0, n_chunks, body, jnp.full((8, LANES), INT_MIN, I32))
        return jnp.max(part, axis=0, keepdims=True)

    def bisect_round(state):
        for _ in range(BISECT_ROUND):
            state = bisect_pass(*state, next_probe(state[0], state[1]))
        return state

    def extract_step(state):
        lo, hi, clo, chi = state
        top = max_below(hi)
        cnt = count_ge(top)
        open_ = hi != lo + 1
        done = jnp.logical_and(open_, cnt >= kfl)
        down = jnp.logical_and(open_, cnt < kfl)
        return (jnp.where(done, top, lo), jnp.where(done, top + 1, jnp.where(down, top, hi)),
                jnp.where(done, cnt, clo), jnp.where(down, cnt, chi))

    def open_need(lo, hi, chi):
        need = jnp.where(hi != lo + 1, kfl - chi, 0.0)
        return jnp.max(need, axis=1, keepdims=True)[0, 0]

    def bisect_cond(carry):
        it, need = carry[0], carry[1]
        return jnp.logical_and(need > 0.0, it < BISECT_MAX_ROUNDS)

    def bisect_body(carry):
        it, need = carry[0], carry[1]
        state = lax.cond(need <= EXTRACT_BELOW, extract_step, bisect_round, tuple(carry[2:]))
        return (it + 1, open_need(state[0], state[1], state[3])) + tuple(state)

    hi = jnp.maximum(jnp.minimum(kmax, INT_MAX - 1) + 1, lo0 + 1)
    first = jnp.where(kmax > INT_MIN + 2 + FIRST_PROBE_GAP, kmax - FIRST_PROBE_GAP, next_probe(lo0, hi))
    state = bisect_pass(lo0, hi, lim_t.astype(F32), jnp.zeros((1, LANES), F32), first)
    _, _, thr, _, clo, chi = lax.while_loop(
        bisect_cond, bisect_body, (jnp.int32(0), open_need(state[0], state[1], state[3])) + tuple(state))

    extra = clo - kfl
    any_tie = jnp.max(extra, axis=1, keepdims=True)[0, 0]

    @pl.when(any_tie > 0.0)
    def _():
        need = jnp.where(extra > 0.0, kfl - chi, jnp.inf)

        def demote(c, seen):
            kch = keyst_ref[c]
            tied = kch == thr
            rank = seen + jnp.dot(tri_ref[...], jnp.where(tied, 1.0, 0.0).astype(BF16),
                                  preferred_element_type=F32)
            kch = jnp.where(jnp.logical_and(tied, rank > need), kch - 1, kch)
            keyst_ref[c] = kch
            for sl in _slabs(kc):
                keys_ref[c, :, sl] = kch[sl, :].T[:tq, :]
            return rank[kc - 1:kc, :]

        lax.fori_loop(0, n_chunks, demote, jnp.zeros((1, LANES), F32))

    thr_rows = jnp.broadcast_to(thr, (LANES, LANES)).T[:tq, :]
    qv = q_ref[0].astype(F32)
    heads_per_group = N_HEADS // N_KV
    rows = heads_per_group * tq
    for g in range(N_KV):
        for r in range(heads_per_group):
            hd = g * heads_per_group + r
            slab = qv[:, (hd // 2) * LANES:(hd // 2 + 1) * LANES]
            keep = (lane < HEAD_DIM) if hd % 2 == 0 else (lane >= HEAD_DIM)
            qg_ref[g, r * tq:(r + 1) * tq, :] = jnp.where(keep, slab, 0.0).astype(BF16)
    m_ref[...] = jnp.full(m_ref.shape, NEG, F32)
    l_ref[...] = jnp.zeros(l_ref.shape, F32)
    acc_ref[...] = jnp.zeros(acc_ref.shape, F32)

    def qk_scores(c, s_ref, mx_ref):
        c = jnp.minimum(c, last_chunk)
        off = pl.multiple_of(c * kc, kc)
        for sl in _slabs(kc):
            bias_ref[:, sl] = jnp.where(keys_ref[c, :, sl] >= thr_rows, 0.0, NEG)
        for g in range(N_KV):
            kch = k2_ref[0, pl.ds(off, kc), g * LANES:(g + 1) * LANES]
            s = lax.dot_general(qg_ref[g], kch, (((1,), (1,)), ((), ())),
                                preferred_element_type=F32)
            for r in range(heads_per_group):
                rs = slice(r * tq, (r + 1) * tq)
                masked = [s[rs, sl] + bias_ref[:, sl] for sl in _slabs(kc)]
                for sl, piece in zip(_slabs(kc), masked):
                    s_ref[g, rs, sl] = piece
                part = _tree(jnp.maximum, masked)
                mx_ref[g, rs, :] = jnp.broadcast_to(jnp.max(part, axis=1, keepdims=True), (tq, LANES))

    def softmax_pv(c, s_ref, mx_ref):
        off = pl.multiple_of(c * kc, kc)
        vch = v_ref[0, pl.ds(off, kc), :]
        for g in range(N_KV):
            m_prev = m_ref[g]
            m_new = jnp.maximum(m_prev, mx_ref[g])
            alpha = jnp.exp2(m_prev - m_new)
            for r in range(heads_per_group):
                rs = slice(r * tq, (r + 1) * tq)
                ps = [jnp.exp2(s_ref[g, rs, sl] - m_new[rs]) for sl in _slabs(kc)]
                p_ref[g, rs, :] = jnp.concatenate(ps, axis=1).astype(BF16)
                l_ref[g, rs, :] = alpha[rs] * l_ref[g, rs, :] + tree_sum(ps)
            acc_ref[g] = alpha * acc_ref[g] + jnp.dot(p_ref[g], vch, preferred_element_type=F32)
            m_ref[g] = m_new

    stage_a = (sa_ref, mxa_ref)
    stage_b = (sb_ref, mxb_ref)
    qk_scores(0, *stage_a)

    def attend_pair(i, carry):
        qk_scores(2 * i + 1, *stage_b)
        softmax_pv(2 * i, *stage_a)
        qk_scores(2 * i + 2, *stage_a)
        softmax_pv(2 * i + 1, *stage_b)
        return carry

    lax.fori_loop(0, n_chunks // 2, attend_pair, 0)

    @pl.when(odd_chunks)
    def _():
        softmax_pv(n_chunks - 1, *stage_a)

    for g in range(N_KV):
        out = acc_ref[g] / jnp.sum(l_ref[g], axis=1, keepdims=True)
        for jj in range(heads_per_group // 2):
            a = out[(2 * jj) * tq:(2 * jj + 1) * tq, :]
            bb = out[(2 * jj + 1) * tq:(2 * jj + 2) * tq, :]
            if g % 2 == 0:
                slab = jnp.where(lane < HEAD_DIM, a, pltpu.roll(bb, HEAD_DIM, axis=1))
            else:
                slab = jnp.where(lane < HEAD_DIM, pltpu.roll(a, HEAD_DIM, axis=1), bb)
            so = (g * heads_per_group // 2 + jj) * LANES
            o_ref[0, :, so:so + LANES] = slab.astype(BF16)


def _attention(q, iqcat, iw, k2, vb, ikcat, *, tq, n_keys, q_pos0):
    b, t, _ = q.shape
    lpad = k2.shape[1]
    kc = KEY_CHUNK
    assert lpad % kc == 0 and t % tq == 0 and tq <= LANES and CHUNK & (CHUNK - 1) == 0
    topk = min(TOPK_MAX, n_keys // 4)
    kern = functools.partial(_attn_kernel, tq=tq, kc=kc, n_keys=n_keys, q_pos0=q_pos0, topk=topk)
    qtile = lambda n: pl.BlockSpec((1, tq, n), lambda i, j: (i, j, 0))
    ktile = lambda n: pl.BlockSpec((1, lpad, n), lambda i, j: (i, 0, 0))
    rows = (N_HEADS // N_KV) * tq
    return pl.pallas_call(
        kern, grid=(b, t // tq),
        in_specs=[qtile(ATTN_DIM), qtile(N_IDX_HEADS * LANES), qtile(LANES),
                  ktile(2 * N_KV * HEAD_DIM), ktile(N_KV * HEAD_DIM), ktile(LANES),
                  pl.BlockSpec((kc, kc), lambda i, j: (0, 0))],
        out_specs=qtile(ATTN_DIM),
        out_shape=jax.ShapeDtypeStruct((b, t, ATTN_DIM), BF16),
        scratch_shapes=[
            pltpu.VMEM((lpad // kc, kc, LANES), I32),
            pltpu.VMEM((lpad // kc, tq, kc), I32),
            pltpu.VMEM((N_IDX_HEADS * tq, kc), F32),
            pltpu.VMEM((N_IDX_HEADS * tq, kc), F32),
            pltpu.VMEM((N_IDX_HEADS * tq, LANES), BF16),
            pltpu.VMEM((N_IDX_HEADS, tq, LANES), F32),
            pltpu.VMEM((N_KV, rows, LANES), BF16),
            pltpu.VMEM((N_KV, rows, kc), F32),
            pltpu.VMEM((N_KV, rows, kc), F32),
            pltpu.VMEM((tq, kc), F32),
            pltpu.VMEM((N_KV, rows, LANES), F32),
            pltpu.VMEM((N_KV, rows, LANES), F32),
            pltpu.VMEM((N_KV, rows, kc), BF16),
            pltpu.VMEM((N_KV, rows, LANES), F32),
            pltpu.VMEM((N_KV, rows, LANES), F32),
            pltpu.VMEM((N_KV, rows, LANES), F32),
        ],
        compiler_params=pltpu.CompilerParams(dimension_semantics=("arbitrary", "arbitrary"),
                                             vmem_limit_bytes=V7X_VMEM_LIMIT),
        name="attention",
    )(q, iqcat, iw, k2, vb, ikcat, jnp.asarray(np.tril(np.ones((kc, kc), np.float32)), BF16))


def _mix_in(x_ref, yc_ref, at_ref, woa_ref, wob_ref, g_ref):
    x1 = (x_ref[...] + jnp.dot(yc_ref[...], woa_ref[...], preferred_element_type=F32)
          + jnp.dot(at_ref[...], wob_ref[...], preferred_element_type=F32))
    return x1, _rmsnorm(x1, g_ref[...])


def _swiglu(hb, wg, wu, wd):
    a = jnp.dot(hb, wg, preferred_element_type=F32)
    b = jnp.dot(hb, wu, preferred_element_type=F32)
    act = (a * (1.0 / (1.0 + jnp.exp(-a))) * b).astype(BF16)
    return jnp.dot(act, wd, preferred_element_type=F32)


def _ffn_dense_kernel(x_ref, yc_ref, at_ref, woa_ref, wob_ref, g_ref, wg_ref, wu_ref, wd_ref, o_ref):
    x1, hn = _mix_in(x_ref, yc_ref, at_ref, woa_ref, wob_ref, g_ref)
    o_ref[...] = x1 + _swiglu(hn.astype(BF16), wg_ref[...], wu_ref[...], wd_ref[...])


def _ffn_dense(x, yc, at, lw, tm):
    m, d = x.shape
    row = lambda n: pl.BlockSpec((tm, n), lambda i: (i, 0))
    consts = [lw["wo_a"], lw["wo_b"], lw["ffn_g"], lw["wg"], lw["wu"], lw["wd"]]
    return pl.pallas_call(
        _ffn_dense_kernel, grid=(m // tm,),
        in_specs=[row(d), row(CONV_DIM), row(ATTN_DIM)]
        + [pl.BlockSpec(c.shape, lambda i: (0, 0), pipeline_mode=pl.Buffered(1)) for c in consts],
        out_specs=row(d), out_shape=jax.ShapeDtypeStruct((m, d), F32),
        compiler_params=pltpu.CompilerParams(dimension_semantics=("arbitrary",),
                                             vmem_limit_bytes=V7X_VMEM_LIMIT),
        name="ffn_dense",
    )(x, yc, at, *consts)


def _ffn_moe_kernel(x_ref, yc_ref, at_ref, woa_ref, wob_ref, g_ref, rhi_ref, rlo_ref, tril_ref,
                    eg_ref, eu_ref, ed_ref, o_ref, h_ref, gate_ref, rank_ref, rankt_ref, cnt_ref, acc_ref,
                    *, cap):
    e = pl.program_id(1)
    tm = x_ref.shape[0]
    lane = lax.broadcasted_iota(I32, (tm, LANES), 1)

    @pl.when(e == 0)
    def _():
        x1, hn = _mix_in(x_ref, yc_ref, at_ref, woa_ref, wob_ref, g_ref)
        hi, lo = _split_hi_lo(hn)
        h_ref[...] = hi
        rhi = rhi_ref[...]
        logits = (jnp.dot(hi, rhi, preferred_element_type=F32)
                  + jnp.dot(lo, rhi, preferred_element_type=F32)
                  + jnp.dot(hi, rlo_ref[...], preferred_element_type=F32))
        logits = jnp.where(lane < N_EXPERTS, logits, -jnp.inf)
        ex = jnp.exp(logits - jnp.max(logits, axis=1, keepdims=True))
        probs = ex / jnp.sum(ex, axis=1, keepdims=True)
        p1 = jnp.max(probs, axis=1, keepdims=True)
        i1 = jnp.min(jnp.where(probs == p1, lane, LANES), axis=1, keepdims=True)
        rest = jnp.where(lane == i1, -1.0, probs)
        p2 = jnp.max(rest, axis=1, keepdims=True)
        i2 = jnp.min(jnp.where(rest == p2, lane, LANES), axis=1, keepdims=True)
        den = p1 + p2
        gate_ref[...] = jnp.where(lane == i1, p1 / den, jnp.where(lane == i2, p2 / den, 0.0))
        routed = jnp.where(jnp.logical_or(lane == i1, lane == i2), 1.0, 0.0)
        before = jnp.dot(tril_ref[...], routed.astype(BF16), preferred_element_type=F32)
        rank = jnp.where(routed > 0.0, before, -1.0)
        rank_ref[...] = rank
        for r0 in range(0, tm, LANES):
            nr = min(LANES, tm - r0)
            blk = rank[r0:r0 + nr, :]
            if nr < LANES:
                blk = jnp.concatenate([blk, jnp.full((LANES - nr, LANES), -1.0, F32)], axis=0)
            rt = blk.T
            for ex_id in range(N_EXPERTS):
                rankt_ref[ex_id, :, r0:r0 + nr] = rt[ex_id:ex_id + 1, :nr]
        cnt_ref[...] = jnp.broadcast_to(jnp.sum(routed, axis=0, keepdims=True), cnt_ref.shape)
        acc_ref[...] = x1

    mine = lane == e
    rank_col = jnp.sum(jnp.where(mine, rank_ref[...], 0.0), axis=1, keepdims=True)
    gate_col = jnp.sum(jnp.where(mine, gate_ref[...], 0.0), axis=1, keepdims=True)
    rank_row = rankt_ref[e]
    n_tok = jnp.sum(jnp.where(mine[:8], cnt_ref[...], 0.0), axis=1, keepdims=True)[0, 0]
    n_sub = (n_tok.astype(I32) + cap - 1) // cap
    kpad = -(-cap // LANES) * LANES
    slot_r = lax.broadcasted_iota(I32, (cap, tm), 0).astype(F32)
    slot_c = lax.broadcasted_iota(I32, (tm, kpad), 1).astype(F32)

    def sub_block(sb, carry):
        base = (sb * cap).astype(F32)
        gather = jnp.where(rank_row - base == slot_r, 1.0, 0.0).astype(BF16)
        hs = jnp.dot(gather, h_ref[...], preferred_element_type=F32).astype(BF16)
        y = _swiglu(hs, eg_ref[0], eu_ref[0], ed_ref[0]).astype(BF16)
        if kpad > cap:
            y = jnp.concatenate([y, jnp.zeros((kpad - cap, y.shape[1]), BF16)], axis=0)
        scatter = jnp.where(rank_col - base == slot_c, 1.0, 0.0).astype(BF16)
        acc_ref[...] += gate_col * jnp.dot(scatter, y, preferred_element_type=F32)
        return carry

    lax.fori_loop(0, n_sub, sub_block, 0)

    @pl.when(e == pl.num_programs(1) - 1)
    def _():
        o_ref[...] = acc_ref[...]


def _ffn_moe(x, yc, at, lw, tm):
    m, d = x.shape
    ne, _, fe = lw["eg"].shape
    cap = min(tm, -(-(9 * tm // 32) // 16) * 16)
    row = lambda n: pl.BlockSpec((tm, n), lambda i, e: (i, 0))
    tril = jnp.asarray(np.tril(np.ones((tm, tm), np.float32), -1), BF16)
    consts = [lw["wo_a"], lw["wo_b"], lw["ffn_g"], lw["r_hi"], lw["r_lo"], tril]
    return pl.pallas_call(
        functools.partial(_ffn_moe_kernel, cap=cap), grid=(m // tm, ne),
        in_specs=[row(d), row(CONV_DIM), row(ATTN_DIM)]
        + [pl.BlockSpec(c.shape, lambda i, e: (0, 0), pipeline_mode=pl.Buffered(1)) for c in consts]
        + [pl.BlockSpec((1, d, fe), lambda i, e: (e, 0, 0)),
           pl.BlockSpec((1, d, fe), lambda i, e: (e, 0, 0)),
           pl.BlockSpec((1, fe, d), lambda i, e: (e, 0, 0))],
        out_specs=row(d), out_shape=jax.ShapeDtypeStruct((m, d), F32),
        scratch_shapes=[pltpu.VMEM((tm, d), BF16),
                        pltpu.VMEM((tm, LANES), F32),
                        pltpu.VMEM((tm, LANES), F32),
                        pltpu.VMEM((ne, 1, tm), F32),
                        pltpu.VMEM((8, LANES), F32),
                        pltpu.VMEM((tm, d), F32)],
        compiler_params=pltpu.CompilerParams(dimension_semantics=("arbitrary", "arbitrary"),
                                             vmem_limit_bytes=V7X_VMEM_LIMIT),
        name="ffn_moe",
    )(x, yc, at, *consts, lw["eg"], lw["eu"], lw["ed"])


def _block_diag_mean(n, group, valid=None):
    m = np.zeros((n, n), np.float32)
    for s in range(0, n if valid is None else valid, group):
        m[s:s + group, s:s + group] = 1.0 / group
    return jnp.asarray(m, BF16)


def _placement_matrices():
    s1 = np.zeros((N_IDX_HEADS * IDX_DIM, N_IDX_HEADS * LANES), np.float32)
    s2 = np.zeros_like(s1)
    t1 = np.zeros((LANES, LANES), np.float32)
    t2 = np.zeros_like(t1)
    j = np.arange(IDX_DIM)
    for h in range(N_IDX_HEADS):
        s1[h * IDX_DIM + j, h * LANES + j] = 1.0
        s2[h * IDX_DIM + j, h * LANES + IDX_DIM + j] = 1.0
        s1[h * IDX_DIM + j, h * LANES + 2 * IDX_DIM + j] = 1.0
    t1[j, j] = 1.0
    t1[j, IDX_DIM + j] = 1.0
    t2[j, 2 * IDX_DIM + j] = 1.0
    return tuple(jnp.asarray(a, BF16) for a in (s1, s2, t1, t2))


def _layer_weights(l, p):
    d = p["w_in"].shape[1]
    s1, s2, t1, t2 = _placement_matrices()
    lw = {
        "attn_g": p["attn_norm_g"][l][None],
        "w_in": jnp.pad(p["w_in"][l], ((0, 0), (0, IN_COLS_PAD - IN_COLS))).astype(BF16),
        "conv_w": p["conv_w"][l],
        "gq": jnp.tile(p["q_norm_g"][l], N_HEADS)[None],
        "gk": jnp.tile(p["k_norm_g"][l], N_KV)[None],
        "giq": jnp.tile(p["iq_norm_g"][l], N_IDX_HEADS)[None],
        "gik": jnp.pad(p["ik_norm_g"][l], (0, LANES - IDX_DIM))[None],
        "mq": _block_diag_mean(ATTN_DIM, HEAD_DIM),
        "mk": _block_diag_mean(N_KV * HEAD_DIM, HEAD_DIM),
        "miq": _block_diag_mean(N_IDX_HEADS * IDX_DIM, IDX_DIM),
        "mik": _block_diag_mean(LANES, IDX_DIM, valid=IDX_DIM),
        "s1": s1, "s2": s2, "t1": t1, "t2": t2,
        "wo_a": p["w_out"][l][:CONV_DIM].astype(BF16),
        "wo_b": p["w_out"][l][CONV_DIM:].astype(BF16),
        "ffn_g": p["ffn_norm_g"][l][None],
    }
    j = l // 2
    if l % 2 == 0:
        lw.update(wg=p["ffn_w_gate"][j].astype(BF16), wu=p["ffn_w_up"][j].astype(BF16),
                  wd=p["ffn_w_down"][j].astype(BF16))
    else:
        r = jnp.pad(p["router_w"][j], ((0, 0), (0, LANES - N_EXPERTS)))
        r_hi = r.astype(BF16)
        lw.update(r_hi=r_hi, r_lo=(r - r_hi.astype(F32)).astype(BF16),
                  eg=p["moe_w_gate"][j].astype(BF16), eu=p["moe_w_up"][j].astype(BF16),
                  ed=p["moe_w_down"][j].astype(BF16))
    assert d == lw["wo_a"].shape[1]
    return lw


def _pick_tile(n, pref):
    t = min(n, pref)
    while n % t:
        t //= 2
    return t


def _mixer(x, prev, lw, *, tq, past_k2=None, past_v=None, past_ik=None, q_pos0=0):
    b, t, d = x.shape
    yc, q, k, v, k2, vb, iqcat, iw, ik, ikcat, cst = _inproj(x, prev, lw, _pick_tile(t, 512))
    if past_k2 is not None:
        k2 = jnp.concatenate([past_k2, k2], axis=1)
        vb = jnp.concatenate([past_v, vb], axis=1)
        ikcat = jnp.concatenate([past_ik, ikcat], axis=1)
    n_keys = k2.shape[1]
    pad = (-n_keys) % KEY_CHUNK
    if pad:
        k2, vb, ikcat = (jnp.pad(a, ((0, 0), (0, pad), (0, 0))) for a in (k2, vb, ikcat))
    at = _attention(q, iqcat, iw, k2, vb, ikcat, tq=tq, n_keys=n_keys, q_pos0=q_pos0)
    m = b * t
    x2, yc2, at2 = x.reshape(m, d), yc.reshape(m, CONV_DIM), at.reshape(m, ATTN_DIM)
    if "wg" in lw:
        y = _ffn_dense(x2, yc2, at2, lw, _pick_tile(m, 512))
    else:
        y = _ffn_moe(x2, yc2, at2, lw, _pick_tile(m, 1024))
    return (y.reshape(b, t, d), k.reshape(b, t, N_KV, HEAD_DIM), v.reshape(b, t, N_KV, HEAD_DIM),
            ik, cst)


def kernel(x_prompt, x_sample, cache_k, cache_v, cache_ik, state_conv, attn_norm_g, w_in, conv_w,
           q_norm_g, k_norm_g, iq_norm_g, ik_norm_g, w_out, ffn_norm_g, ffn_w_gate, ffn_w_up,
           ffn_w_down, router_w, moe_w_gate, moe_w_up, moe_w_down):
    params = dict(attn_norm_g=attn_norm_g, w_in=w_in, conv_w=conv_w, q_norm_g=q_norm_g,
                  k_norm_g=k_norm_g, iq_norm_g=iq_norm_g, ik_norm_g=ik_norm_g, w_out=w_out,
                  ffn_norm_g=ffn_norm_g, ffn_w_gate=ffn_w_gate, ffn_w_up=ffn_w_up,
                  ffn_w_down=ffn_w_down, router_w=router_w, moe_w_gate=moe_w_gate,
                  moe_w_up=moe_w_up, moe_w_down=moe_w_down)
    depth = w_in.shape[0]
    xp, xs = x_prompt, x_sample
    bp = xp.shape[0]
    bs, ts, _ = xs.shape
    past = cache_k.shape[2]
    kvd = N_KV * HEAD_DIM
    outs = [[] for _ in range(8)]
    for l in range(depth):
        lw = _layer_weights(l, params)
        prev0 = jnp.zeros((bp, CONV_W - 1, CONV_DIM), F32)
        xp, kp, vp, ikp, cp = _mixer(xp, prev0, lw, tq=Q_BLOCK)

        ck = cache_k[l].reshape(bs, past, kvd)
        g0, g1 = ck[..., :HEAD_DIM], ck[..., HEAD_DIM:]
        past_k2 = jnp.concatenate([g0, g0, g1, g1], axis=-1).astype(BF16)
        past_v = cache_v[l].reshape(bs, past, kvd).astype(BF16)
        cik = cache_ik[l]
        cik_hi = cik.astype(BF16)
        cik_lo = (cik - cik_hi.astype(F32)).astype(BF16)
        past_ik = jnp.concatenate([cik_hi, cik_hi, cik_lo, jnp.zeros_like(cik_hi)], axis=-1)
        xs, ks, vs, iks, cs = _mixer(xs, state_conv[l], lw, tq=ts, past_k2=past_k2,
                                     past_v=past_v, past_ik=past_ik, q_pos0=past)
        for lst, val in zip(outs, (kp, vp, ikp, cp, ks, vs, iks, cs)):
            lst.append(val)
    kp, vp, ikp, cp, ks, vs, iks, cs = (jnp.stack(o) for o in outs)
    return (xp, xs, kp, vp, ikp, cp, ks, vs, iks, cs)
```

```python
import functools
import math

import numpy as np
import jax
import jax.numpy as jnp
from jax import lax
from jax.experimental import pallas as pl
from jax.experimental.pallas import tpu as pltpu

CHUNK = 64
CONV_DIM = 512
CONV_W = 3
N_HEADS = 8
HEAD_DIM = 64
ATTN_DIM = N_HEADS * HEAD_DIM
N_KV = 2
N_IDX_HEADS = 8
IDX_DIM = 32
TOPK_MAX = 256
Q_BLOCK = 128
N_EXPERTS = 8
EPS = 1e-6
COL_SIZES = (CONV_DIM, CONV_DIM, CONV_DIM, ATTN_DIM, N_KV * HEAD_DIM, N_KV * HEAD_DIM,
             N_IDX_HEADS * IDX_DIM, IDX_DIM, N_IDX_HEADS)
IN_COLS = sum(COL_SIZES)

LANES = 128
V7X_VMEM_LIMIT = 56 * 1024 * 1024

IN_COLS_PAD = ((IN_COLS + LANES - 1) // LANES) * LANES
TAIL_OFF = IN_COLS_PAD - LANES
IW_LANE = IDX_DIM
QSCALE = HEAD_DIM ** -0.5 * math.log2(math.e)
IWSCALE = IDX_DIM ** -0.5 * N_IDX_HEADS ** -0.5
NEG = -1e30
INT_MIN = -2 ** 31
INT_MAX = 2 ** 31 - 1
KEY_CHUNK = 512
BISECT_ROUND = 4
BISECT_FREE_PASSES = 8
EXTRACT_BELOW = 2
BISECT_MAX_ROUNDS = -(-(32 + 3) // BISECT_ROUND) + EXTRACT_BELOW
FIRST_PROBE_GAP = 3 << 23

F32 = jnp.float32
BF16 = jnp.bfloat16
I32 = jnp.int32


def _split_hi_lo(x):
    hi = x.astype(BF16)
    lo = (x - hi.astype(F32)).astype(BF16)
    return hi, lo


def _group_rsqrt(x, g_ref):
    hi, lo = _split_hi_lo(x * x)
    g = g_ref[...]
    ms = jnp.dot(hi, g, preferred_element_type=F32) + jnp.dot(lo, g, preferred_element_type=F32)
    return lax.rsqrt(ms + EPS)


def _rmsnorm(x, g):
    return x * lax.rsqrt(jnp.mean(x * x, axis=-1, keepdims=True) + EPS) * g


def _inproj_kernel(x_ref, prev_ref, g_ref, w_ref, cw_ref, gq_ref, gk_ref, giq_ref, gik_ref,
                   mq_ref, mk_ref, miq_ref, mik_ref, s1_ref, s2_ref, t1_ref, t2_ref,
                   yconv_ref, q_ref, k_ref, v_ref, k2_ref, vb_ref, iqcat_ref, iw_ref, ik_ref,
                   ikcat_ref, cst_ref, carry_ref):
    t = pl.program_id(1)
    nt = pl.num_programs(1)
    x = x_ref[0]
    tm = x.shape[0]
    h = _rmsnorm(x, g_ref[...]).astype(BF16)
    p = jnp.dot(h, w_ref[...], preferred_element_type=F32)
    o = np.cumsum((0,) + COL_SIZES)
    c_gate, b_gate, u = p[:, o[0]:o[1]], p[:, o[1]:o[2]], p[:, o[2]:o[3]]
    q, k, v, iq = p[:, o[3]:o[4]], p[:, o[4]:o[5]], p[:, o[5]:o[6]], p[:, o[6]:o[7]]
    tail = p[:, TAIL_OFF:]

    ci = c_gate * u

    @pl.when(t == 0)
    def _():
        carry_ref[8 - (CONV_W - 1):, :] = prev_ref[0]

    row = lax.broadcasted_iota(I32, ci.shape, 0)
    c7 = carry_ref[7:8, :]
    c6 = carry_ref[6:7, :]
    s1 = jnp.where(row == 0, c7, pltpu.roll(ci, 1, axis=0))
    s2 = jnp.where(row == 0, c6, jnp.where(row == 1, c7, pltpu.roll(ci, 2, axis=0)))
    cw = cw_ref[...]
    conv = cw[0:1] * s2 + cw[1:2] * s1 + cw[2:3] * ci
    yconv_ref[0] = (b_gate * conv).astype(BF16)
    carry_ref[...] = ci[tm - 8:, :]

    @pl.when(t == nt - 1)
    def _():
        cst_ref[0] = ci[tm - (CONV_W - 1):, :]

    qn = q * _group_rsqrt(q, mq_ref) * gq_ref[...]
    q_ref[0] = (qn * QSCALE).astype(BF16)
    kn = k * _group_rsqrt(k, mk_ref) * gk_ref[...]
    k_ref[0] = kn
    v_ref[0] = v
    lane = lax.broadcasted_iota(I32, kn.shape, 1)
    kr = pltpu.roll(kn, HEAD_DIM, axis=1)
    k2_ref[0, :, 0:LANES] = jnp.where(lane < HEAD_DIM, kn, kr).astype(BF16)
    k2_ref[0, :, LANES:2 * LANES] = jnp.where(lane < HEAD_DIM, kr, kn).astype(BF16)
    vb_ref[0] = v.astype(BF16)

    iqn = iq * _group_rsqrt(iq, miq_ref) * giq_ref[...]
    ihi, ilo = _split_hi_lo(iqn)
    iqcat_ref[0] = (jnp.dot(ihi, s1_ref[...], preferred_element_type=F32)
                    + jnp.dot(ilo, s2_ref[...], preferred_element_type=F32)).astype(BF16)
    tn = tail * _group_rsqrt(tail, mik_ref) * gik_ref[...]
    ik_ref[0] = tn[:, 0:IDX_DIM]
    thi, tlo = _split_hi_lo(tn)
    ikcat_ref[0] = (jnp.dot(thi, t1_ref[...], preferred_element_type=F32)
                    + jnp.dot(tlo, t2_ref[...], preferred_element_type=F32)).astype(BF16)
    iw_ref[0] = tail * IWSCALE


def _const_spec(shape):
    nd = len(shape)
    return pl.BlockSpec(shape, lambda *_: (0,) * nd)


def _inproj(x, prev, lw, tm):
    b, t, d = x.shape
    grid = (b, t // tm)
    tile = lambda n: pl.BlockSpec((1, tm, n), lambda i, j: (i, j, 0))
    consts = [lw["attn_g"], lw["w_in"], lw["conv_w"], lw["gq"], lw["gk"], lw["giq"], lw["gik"],
              lw["mq"], lw["mk"], lw["miq"], lw["mik"], lw["s1"], lw["s2"], lw["t1"], lw["t2"]]
    in_specs = ([tile(d), pl.BlockSpec((1, CONV_W - 1, CONV_DIM), lambda i, j: (i, 0, 0))]
                + [_const_spec(c.shape) for c in consts])
    kvd = N_KV * HEAD_DIM
    out_shape = (
        jax.ShapeDtypeStruct((b, t, CONV_DIM), BF16),
        jax.ShapeDtypeStruct((b, t, ATTN_DIM), BF16),
        jax.ShapeDtypeStruct((b, t, kvd), F32),
        jax.ShapeDtypeStruct((b, t, kvd), F32),
        jax.ShapeDtypeStruct((b, t, 2 * kvd), BF16),
        jax.ShapeDtypeStruct((b, t, kvd), BF16),
        jax.ShapeDtypeStruct((b, t, N_IDX_HEADS * LANES), BF16),
        jax.ShapeDtypeStruct((b, t, LANES), F32),
        jax.ShapeDtypeStruct((b, t, IDX_DIM), F32),
        jax.ShapeDtypeStruct((b, t, LANES), BF16),
        jax.ShapeDtypeStruct((b, CONV_W - 1, CONV_DIM), F32),
    )
    out_specs = (tile(CONV_DIM), tile(ATTN_DIM), tile(kvd), tile(kvd), tile(2 * kvd), tile(kvd),
                 tile(N_IDX_HEADS * LANES), tile(LANES), tile(IDX_DIM), tile(LANES),
                 pl.BlockSpec((1, CONV_W - 1, CONV_DIM), lambda i, j: (i, 0, 0)))
    return pl.pallas_call(
        _inproj_kernel, grid=grid, in_specs=in_specs, out_specs=out_specs, out_shape=out_shape,
        scratch_shapes=[pltpu.VMEM((8, CONV_DIM), F32)],
        compiler_params=pltpu.CompilerParams(dimension_semantics=("arbitrary", "arbitrary"),
                                             vmem_limit_bytes=V7X_VMEM_LIMIT),
        name="inproj",
    )(x, prev, *consts)


def _slabs(n):
    return [slice(j * LANES, (j + 1) * LANES) for j in range(n // LANES)]


def _tree(op, xs):
    while len(xs) > 1:
        xs = [op(xs[i], xs[i + 1]) for i in range(0, len(xs) - 1, 2)] + ([xs[-1]] if len(xs) % 2 else [])
    return xs[0]


def _attn_kernel(q_ref, iqcat_ref, iw_ref, k2_ref, v_ref, ikcat_ref, tri_ref, o_ref,
                 keyst_ref, keys_ref, lga_ref, lgb_ref, lhs_ref, wb_ref, qg_ref, sa_ref, sb_ref,
                 bias_ref, mxa_ref, mxb_ref, p_ref, m_ref, l_ref, acc_ref,
                 *, tq, kc, n_keys, q_pos0, topk):
    qi = pl.program_id(1)
    q0 = q_pos0 + qi * tq
    shift = CHUNK.bit_length() - 1

    def key_limit(pos):
        return jnp.minimum(((pos >> shift) + 1) << shift, n_keys)

    lane = lax.broadcasted_iota(I32, (tq, LANES), 1)
    lim = key_limit(q0 + lax.broadcasted_iota(I32, (tq, LANES), 0))
    qlane = lax.broadcasted_iota(I32, (1, LANES), 1)
    lim_t = jnp.where(qlane < tq, key_limit(q0 + qlane), 0)
    n_chunks = (key_limit(q0 + tq - 1) + kc - 1) // kc
    kfl = jnp.minimum(lim_t, topk).astype(F32)

    iqc = iqcat_ref[0]
    for h in range(N_IDX_HEADS):
        lhs_ref[h * tq:(h + 1) * tq, :] = iqc[:, h * LANES:(h + 1) * LANES]
    iw = iw_ref[0]
    for h in range(N_IDX_HEADS):
        col = jnp.sum(jnp.where(lane == IW_LANE + h, iw, 0.0), axis=1, keepdims=True)
        wb_ref[h] = jnp.broadcast_to(col, (tq, LANES))
    pad_rows = jnp.full((LANES - tq, LANES), INT_MIN, I32) if tq < LANES else None

    last_chunk = keys_ref.shape[0] - 1

    def logits(c, lg_ref):
        off = pl.multiple_of(jnp.minimum(c, last_chunk) * kc, kc)
        lg_ref[...] = lax.dot_general(lhs_ref[...], ikcat_ref[0, pl.ds(off, kc), :],
                                      (((1,), (1,)), ((), ())), preferred_element_type=F32)

    def keys_from(c, lg_ref, kmax):
        for j, sl in enumerate(_slabs(kc)):
            sc = None
            for h in range(N_IDX_HEADS):
                term = jnp.maximum(lg_ref[h * tq:(h + 1) * tq, sl], 0.0) * wb_ref[h]
                sc = term if sc is None else sc + term
            sc = sc + 0.0
            bits = lax.bitcast_convert_type(sc, I32)
            key = bits ^ ((bits >> 31) & INT_MAX)
            key = jnp.where(c * kc + j * LANES + lane < lim, key, INT_MIN)
            keys_ref[c, :, sl] = key
            if pad_rows is not None:
                key = jnp.concatenate([key, pad_rows], axis=0)
            kt = key.T
            keyst_ref[c, sl, :] = kt
            kmax = jnp.maximum(kmax, _tree(jnp.maximum, [kt[8 * i:8 * i + 8, :]
                                                         for i in range(LANES // 8)]))
        return kmax

    logits(0, lga_ref)

    def score_pair(i, kmax):
        logits(2 * i + 1, lgb_ref)
        kmax = keys_from(2 * i, lga_ref, kmax)
        logits(2 * i + 2, lga_ref)
        return keys_from(2 * i + 1, lgb_ref, kmax)

    odd_chunks = n_chunks % 2 == 1
    kmax = lax.fori_loop(0, n_chunks // 2, score_pair, jnp.full((8, LANES), INT_MIN, I32))
    kmax = lax.cond(odd_chunks, lambda k: keys_from(n_chunks - 1, lga_ref, k), lambda k: k, kmax)
    kmax = jnp.max(kmax, axis=0, keepdims=True)

    tree_sum = functools.partial(_tree, jnp.add)

    def count_ge(thr):
        def body(c, acc):
            hit = jnp.where(keyst_ref[c] >= thr, 1.0, 0.0)
            return acc + tree_sum([hit[8 * i:8 * i + 8, :] for i in range(kc // 8)])

        part = lax.fori_loop(0, n_chunks, body, jnp.zeros((8, LANES), F32))
        return jnp.sum(part, axis=0, keepdims=True)

    lo0 = jnp.full((1, LANES), INT_MIN + 1, I32)

    def next_probe(lo, hi):
        mid = (lo & hi) + ((lo ^ hi) >> 1)
        floor = lo == lo0
        return jnp.where(jnp.logical_and(floor, hi > 1), 1,
                         jnp.where(jnp.logical_and(floor, hi == 1), 0, mid))

    def bisect_pass(lo, hi, clo, chi, mid):
        cnt = count_ge(mid)
        ge = cnt >= kfl
        exact = cnt == kfl
        lo_n = jnp.where(ge, mid, lo)
        hi_n = jnp.where(exact, mid + 1, jnp.where(ge, hi, mid))
        return lo_n, hi_n, jnp.where(ge, cnt, clo), jnp.where(ge, chi, cnt)

    def max_below(bound):
        def body(c, acc):
            kch = keyst_ref[c]
            cand = jnp.where(kch < bound, kch, INT_MIN)
            return jnp.maximum(acc, _tree(jnp.maximum, [cand[8 * i:8 * i + 8, :] for i in range(kc // 8)]))

        part = lax.fori_loop(0, n_chunks, body, jnp.full((8, LANES), INT_MIN, I32))
        return jnp.max(part, axis=0, keepdims=True)

    def bisect_round(state):
        for _ in range(BISECT_ROUND):
            state = bisect_pass(*state, next_probe(state[0], state[1]))
        return state

    def extract_step(state):
        lo, hi, clo, chi = state
        top = max_below(hi)
        cnt = count_ge(top)
        open_ = hi != lo + 1
        done = jnp.logical_and(open_, cnt >= kfl)
        down = jnp.logical_and(open_, cnt < kfl)
        return (jnp.where(done, top, lo), jnp.where(done, top + 1, jnp.where(down, top, hi)),
                jnp.where(done, cnt, clo), jnp.where(down, cnt, chi))

    def open_need(lo, hi, chi):
        need = jnp.where(hi != lo + 1, kfl - chi, 0.0)
        return jnp.max(need, axis=1, keepdims=True)[0, 0]

    def bisect_cond(carry):
        it, need = carry[0], carry[1]
        return jnp.logical_and(need > 0.0, it < BISECT_MAX_ROUNDS)

    def bisect_body(carry):
        it, need = carry[0], carry[1]
        state = lax.cond(need <= EXTRACT_BELOW, extract_step, bisect_round, tuple(carry[2:]))
        return (it + 1, open_need(state[0], state[1], state[3])) + tuple(state)

    hi = jnp.maximum(jnp.minimum(kmax, INT_MAX - 1) + 1, lo0 + 1)
    first = jnp.where(kmax > INT_MIN + 2 + FIRST_PROBE_GAP, kmax - FIRST_PROBE_GAP, next_probe(lo0, hi))
    state = bisect_pass(lo0, hi, lim_t.astype(F32), jnp.zeros((1, LANES), F32), first)
    for _ in range(BISECT_FREE_PASSES):
        state = bisect_pass(*state, next_probe(state[0], state[1]))
    _, _, thr, _, clo, chi = lax.while_loop(
        bisect_cond, bisect_body, (jnp.int32(0), open_need(state[0], state[1], state[3])) + tuple(state))

    extra = clo - kfl
    any_tie = jnp.max(extra, axis=1, keepdims=True)[0, 0]

    @pl.when(any_tie > 0.0)
    def _():
        need = jnp.where(extra > 0.0, kfl - chi, jnp.inf)

        def demote(c, seen):
            kch = keyst_ref[c]
            tied = kch == thr
            rank = seen + jnp.dot(tri_ref[...], jnp.where(tied, 1.0, 0.0).astype(BF16),
                                  preferred_element_type=F32)
            kch = jnp.where(jnp.logical_and(tied, rank > need), kch - 1, kch)
            keyst_ref[c] = kch
            for sl in _slabs(kc):
                keys_ref[c, :, sl] = kch[sl, :].T[:tq, :]
            return rank[kc - 1:kc, :]

        lax.fori_loop(0, n_chunks, demote, jnp.zeros((1, LANES), F32))

    thr_rows = jnp.broadcast_to(thr, (LANES, LANES)).T[:tq, :]
    qv = q_ref[0].astype(F32)
    heads_per_group = N_HEADS // N_KV
    rows = heads_per_group * tq
    for g in range(N_KV):
        for r in range(heads_per_group):
            hd = g * heads_per_group + r
            slab = qv[:, (hd // 2) * LANES:(hd // 2 + 1) * LANES]
            keep = (lane < HEAD_DIM) if hd % 2 == 0 else (lane >= HEAD_DIM)
            qg_ref[g, r * tq:(r + 1) * tq, :] = jnp.where(keep, slab, 0.0).astype(BF16)
    m_ref[...] = jnp.full(m_ref.shape, NEG, F32)
    l_ref[...] = jnp.zeros(l_ref.shape, F32)
    acc_ref[...] = jnp.zeros(acc_ref.shape, F32)

    def qk_scores(c, s_ref, mx_ref):
        c = jnp.minimum(c, last_chunk)
        off = pl.multiple_of(c * kc, kc)
        for sl in _slabs(kc):
            bias_ref[:, sl] = jnp.where(keys_ref[c, :, sl] >= thr_rows, 0.0, NEG)
        for g in range(N_KV):
            kch = k2_ref[0, pl.ds(off, kc), g * LANES:(g + 1) * LANES]
            s = lax.dot_general(qg_ref[g], kch, (((1,), (1,)), ((), ())),
                                preferred_element_type=F32)
            for r in range(heads_per_group):
                rs = slice(r * tq, (r + 1) * tq)
                masked = [s[rs, sl] + bias_ref[:, sl] for sl in _slabs(kc)]
                for sl, piece in zip(_slabs(kc), masked):
                    s_ref[g, rs, sl] = piece
                part = _tree(jnp.maximum, masked)
                mx_ref[g, rs, :] = jnp.broadcast_to(jnp.max(part, axis=1, keepdims=True), (tq, LANES))

    def softmax_pv(c, s_ref, mx_ref):
        off = pl.multiple_of(c * kc, kc)
        vch = v_ref[0, pl.ds(off, kc), :]
        for g in range(N_KV):
            m_prev = m_ref[g]
            m_new = jnp.maximum(m_prev, mx_ref[g])
            alpha = jnp.exp2(m_prev - m_new)
            for r in range(heads_per_group):
                rs = slice(r * tq, (r + 1) * tq)
                ps = [jnp.exp2(s_ref[g, rs, sl] - m_new[rs]) for sl in _slabs(kc)]
                p_ref[g, rs, :] = jnp.concatenate(ps, axis=1).astype(BF16)
                l_ref[g, rs, :] = alpha[rs] * l_ref[g, rs, :] + tree_sum(ps)
            acc_ref[g] = alpha * acc_ref[g] + jnp.dot(p_ref[g], vch, preferred_element_type=F32)
            m_ref[g] = m_new

    stage_a = (sa_ref, mxa_ref)
    stage_b = (sb_ref, mxb_ref)
    qk_scores(0, *stage_a)

    def attend_pair(i, carry):
        qk_scores(2 * i + 1, *stage_b)
        softmax_pv(2 * i, *stage_a)
        qk_scores(2 * i + 2, *stage_a)
        softmax_pv(2 * i + 1, *stage_b)
        return carry

    lax.fori_loop(0, n_chunks // 2, attend_pair, 0)

    @pl.when(odd_chunks)
    def _():
        softmax_pv(n_chunks - 1, *stage_a)

    for g in range(N_KV):
        out = acc_ref[g] / jnp.sum(l_ref[g], axis=1, keepdims=True)
        for jj in range(heads_per_group // 2):
            a = out[(2 * jj) * tq:(2 * jj + 1) * tq, :]
            bb = out[(2 * jj + 1) * tq:(2 * jj + 2) * tq, :]
            if g % 2 == 0:
                slab = jnp.where(lane < HEAD_DIM, a, pltpu.roll(bb, HEAD_DIM, axis=1))
            else:
                slab = jnp.where(lane < HEAD_DIM, pltpu.roll(a, HEAD_DIM, axis=1), bb)
            so = (g * heads_per_group // 2 + jj) * LANES
            o_ref[0, :, so:so + LANES] = slab.astype(BF16)


def _attention(q, iqcat, iw, k2, vb, ikcat, *, tq, n_keys, q_pos0):
    b, t, _ = q.shape
    lpad = k2.shape[1]
    kc = KEY_CHUNK
    assert lpad % kc == 0 and t % tq == 0 and tq <= LANES and CHUNK & (CHUNK - 1) == 0
    topk = min(TOPK_MAX, n_keys // 4)
    kern = functools.partial(_attn_kernel, tq=tq, kc=kc, n_keys=n_keys, q_pos0=q_pos0, topk=topk)
    qtile = lambda n: pl.BlockSpec((1, tq, n), lambda i, j: (i, j, 0))
    ktile = lambda n: pl.BlockSpec((1, lpad, n), lambda i, j: (i, 0, 0))
    rows = (N_HEADS // N_KV) * tq
    return pl.pallas_call(
        kern, grid=(b, t // tq),
        in_specs=[qtile(ATTN_DIM), qtile(N_IDX_HEADS * LANES), qtile(LANES),
                  ktile(2 * N_KV * HEAD_DIM), ktile(N_KV * HEAD_DIM), ktile(LANES),
                  pl.BlockSpec((kc, kc), lambda i, j: (0, 0))],
        out_specs=qtile(ATTN_DIM),
        out_shape=jax.ShapeDtypeStruct((b, t, ATTN_DIM), BF16),
        scratch_shapes=[
            pltpu.VMEM((lpad // kc, kc, LANES), I32),
            pltpu.VMEM((lpad // kc, tq, kc), I32),
            pltpu.VMEM((N_IDX_HEADS * tq, kc), F32),
            pltpu.VMEM((N_IDX_HEADS * tq, kc), F32),
            pltpu.VMEM((N_IDX_HEADS * tq, LANES), BF16),
            pltpu.VMEM((N_IDX_HEADS, tq, LANES), F32),
            pltpu.VMEM((N_KV, rows, LANES), BF16),
            pltpu.VMEM((N_KV, rows, kc), F32),
            pltpu.VMEM((N_KV, rows, kc), F32),
            pltpu.VMEM((tq, kc), F32),
            pltpu.VMEM((N_KV, rows, LANES), F32),
            pltpu.VMEM((N_KV, rows, LANES), F32),
            pltpu.VMEM((N_KV, rows, kc), BF16),
            pltpu.VMEM((N_KV, rows, LANES), F32),
            pltpu.VMEM((N_KV, rows, LANES), F32),
            pltpu.VMEM((N_KV, rows, LANES), F32),
        ],
        compiler_params=pltpu.CompilerParams(dimension_semantics=("arbitrary", "arbitrary"),
                                             vmem_limit_bytes=V7X_VMEM_LIMIT),
        name="attention",
    )(q, iqcat, iw, k2, vb, ikcat, jnp.asarray(np.tril(np.ones((kc, kc), np.float32)), BF16))


def _mix_in(x_ref, yc_ref, at_ref, woa_ref, wob_ref, g_ref):
    x1 = (x_ref[...] + jnp.dot(yc_ref[...], woa_ref[...], preferred_element_type=F32)
          + jnp.dot(at_ref[...], wob_ref[...], preferred_element_type=F32))
    return x1, _rmsnorm(x1, g_ref[...])


def _swiglu(hb, wg, wu, wd):
    a = jnp.dot(hb, wg, preferred_element_type=F32)
    b = jnp.dot(hb, wu, preferred_element_type=F32)
    act = (a * (1.0 / (1.0 + jnp.exp(-a))) * b).astype(BF16)
    return jnp.dot(act, wd, preferred_element_type=F32)


def _ffn_dense_kernel(x_ref, yc_ref, at_ref, woa_ref, wob_ref, g_ref, wg_ref, wu_ref, wd_ref, o_ref):
    x1, hn = _mix_in(x_ref, yc_ref, at_ref, woa_ref, wob_ref, g_ref)
    o_ref[...] = x1 + _swiglu(hn.astype(BF16), wg_ref[...], wu_ref[...], wd_ref[...])


def _ffn_dense(x, yc, at, lw, tm):
    m, d = x.shape
    row = lambda n: pl.BlockSpec((tm, n), lambda i: (i, 0))
    consts = [lw["wo_a"], lw["wo_b"], lw["ffn_g"], lw["wg"], lw["wu"], lw["wd"]]
    return pl.pallas_call(
        _ffn_dense_kernel, grid=(m // tm,),
        in_specs=[row(d), row(CONV_DIM), row(ATTN_DIM)]
        + [pl.BlockSpec(c.shape, lambda i: (0, 0), pipeline_mode=pl.Buffered(1)) for c in consts],
        out_specs=row(d), out_shape=jax.ShapeDtypeStruct((m, d), F32),
        compiler_params=pltpu.CompilerParams(dimension_semantics=("arbitrary",),
                                             vmem_limit_bytes=V7X_VMEM_LIMIT),
        name="ffn_dense",
    )(x, yc, at, *consts)


def _ffn_moe_kernel(x_ref, yc_ref, at_ref, woa_ref, wob_ref, g_ref, rhi_ref, rlo_ref, tril_ref,
                    eg_ref, eu_ref, ed_ref, o_ref, h_ref, gate_ref, rank_ref, rankt_ref, cnt_ref, acc_ref,
                    *, cap):
    e = pl.program_id(1)
    tm = x_ref.shape[0]
    lane = lax.broadcasted_iota(I32, (tm, LANES), 1)

    @pl.when(e == 0)
    def _():
        x1, hn = _mix_in(x_ref, yc_ref, at_ref, woa_ref, wob_ref, g_ref)
        hi, lo = _split_hi_lo(hn)
        h_ref[...] = hi
        rhi = rhi_ref[...]
        logits = (jnp.dot(hi, rhi, preferred_element_type=F32)
                  + jnp.dot(lo, rhi, preferred_element_type=F32)
                  + jnp.dot(hi, rlo_ref[...], preferred_element_type=F32))
        logits = jnp.where(lane < N_EXPERTS, logits, -jnp.inf)
        ex = jnp.exp(logits - jnp.max(logits, axis=1, keepdims=True))
        probs = ex / jnp.sum(ex, axis=1, keepdims=True)
        p1 = jnp.max(probs, axis=1, keepdims=True)
        i1 = jnp.min(jnp.where(probs == p1, lane, LANES), axis=1, keepdims=True)
        rest = jnp.where(lane == i1, -1.0, probs)
        p2 = jnp.max(rest, axis=1, keepdims=True)
        i2 = jnp.min(jnp.where(rest == p2, lane, LANES), axis=1, keepdims=True)
        den = p1 + p2
        gate_ref[...] = jnp.where(lane == i1, p1 / den, jnp.where(lane == i2, p2 / den, 0.0))
        routed = jnp.where(jnp.logical_or(lane == i1, lane == i2), 1.0, 0.0)
        before = jnp.dot(tril_ref[...], routed.astype(BF16), preferred_element_type=F32)
        rank = jnp.where(routed > 0.0, before, -1.0)
        rank_ref[...] = rank
        for r0 in range(0, tm, LANES):
            nr = min(LANES, tm - r0)
            blk = rank[r0:r0 + nr, :]
            if nr < LANES:
                blk = jnp.concatenate([blk, jnp.full((LANES - nr, LANES), -1.0, F32)], axis=0)
            rt = blk.T
            for ex_id in range(N_EXPERTS):
                rankt_ref[ex_id, :, r0:r0 + nr] = rt[ex_id:ex_id + 1, :nr]
        cnt_ref[...] = jnp.broadcast_to(jnp.sum(routed, axis=0, keepdims=True), cnt_ref.shape)
        acc_ref[...] = x1

    mine = lane == e
    rank_col = jnp.sum(jnp.where(mine, rank_ref[...], 0.0), axis=1, keepdims=True)
    gate_col = jnp.sum(jnp.where(mine, gate_ref[...], 0.0), axis=1, keepdims=True)
    rank_row = rankt_ref[e]
    n_tok = jnp.sum(jnp.where(mine[:8], cnt_ref[...], 0.0), axis=1, keepdims=True)[0, 0]
    n_sub = (n_tok.astype(I32) + cap - 1) // cap
    kpad = -(-cap // LANES) * LANES
    slot_r = lax.broadcasted_iota(I32, (cap, tm), 0).astype(F32)
    slot_c = lax.broadcasted_iota(I32, (tm, kpad), 1).astype(F32)

    def sub_block(sb, carry):
        base = (sb * cap).astype(F32)
        gather = jnp.where(rank_row - base == slot_r, 1.0, 0.0).astype(BF16)
        hs = jnp.dot(gather, h_ref[...], preferred_element_type=F32).astype(BF16)
        y = _swiglu(hs, eg_ref[0], eu_ref[0], ed_ref[0]).astype(BF16)
        if kpad > cap:
            y = jnp.concatenate([y, jnp.zeros((kpad - cap, y.shape[1]), BF16)], axis=0)
        scatter = jnp.where(rank_col - base == slot_c, 1.0, 0.0).astype(BF16)
        acc_ref[...] += gate_col * jnp.dot(scatter, y, preferred_element_type=F32)
        return carry

    lax.fori_loop(0, n_sub, sub_block, 0)

    @pl.when(e == pl.num_programs(1) - 1)
    def _():
        o_ref[...] = acc_ref[...]


def _ffn_moe(x, yc, at, lw, tm):
    m, d = x.shape
    ne, _, fe = lw["eg"].shape
    cap = min(tm, -(-(9 * tm // 32) // 16) * 16)
    row = lambda n: pl.BlockSpec((tm, n), lambda i, e: (i, 0))
    tril = jnp.asarray(np.tril(np.ones((tm, tm), np.float32), -1), BF16)
    consts = [lw["wo_a"], lw["wo_b"], lw["ffn_g"], lw["r_hi"], lw["r_lo"], tril]
    return pl.pallas_call(
        functools.partial(_ffn_moe_kernel, cap=cap), grid=(m // tm, ne),
        in_specs=[row(d), row(CONV_DIM), row(ATTN_DIM)]
        + [pl.BlockSpec(c.shape, lambda i, e: (0, 0), pipeline_mode=pl.Buffered(1)) for c in consts]
        + [pl.BlockSpec((1, d, fe), lambda i, e: (e, 0, 0)),
           pl.BlockSpec((1, d, fe), lambda i, e: (e, 0, 0)),
           pl.BlockSpec((1, fe, d), lambda i, e: (e, 0, 0))],
        out_specs=row(d), out_shape=jax.ShapeDtypeStruct((m, d), F32),
        scratch_shapes=[pltpu.VMEM((tm, d), BF16),
                        pltpu.VMEM((tm, LANES), F32),
                        pltpu.VMEM((tm, LANES), F32),
                        pltpu.VMEM((ne, 1, tm), F32),
                        pltpu.VMEM((8, LANES), F32),
                        pltpu.VMEM((tm, d), F32)],
        compiler_params=pltpu.CompilerParams(dimension_semantics=("arbitrary", "arbitrary"),
                                             vmem_limit_bytes=V7X_VMEM_LIMIT),
        name="ffn_moe",
    )(x, yc, at, *consts, lw["eg"], lw["eu"], lw["ed"])


def _block_diag_mean(n, group, valid=None):
    m = np.zeros((n, n), np.float32)
    for s in range(0, n if valid is None else valid, group):
        m[s:s + group, s:s + group] = 1.0 / group
    return jnp.asarray(m, BF16)


def _placement_matrices():
    s1 = np.zeros((N_IDX_HEADS * IDX_DIM, N_IDX_HEADS * LANES), np.float32)
    s2 = np.zeros_like(s1)
    t1 = np.zeros((LANES, LANES), np.float32)
    t2 = np.zeros_like(t1)
    j = np.arange(IDX_DIM)
    for h in range(N_IDX_HEADS):
        s1[h * IDX_DIM + j, h * LANES + j] = 1.0
        s2[h * IDX_DIM + j, h * LANES + IDX_DIM + j] = 1.0
        s1[h * IDX_DIM + j, h * LANES + 2 * IDX_DIM + j] = 1.0
    t1[j, j] = 1.0
    t1[j, IDX_DIM + j] = 1.0
    t2[j, 2 * IDX_DIM + j] = 1.0
    return tuple(jnp.asarray(a, BF16) for a in (s1, s2, t1, t2))


def _layer_weights(l, p):
    d = p["w_in"].shape[1]
    s1, s2, t1, t2 = _placement_matrices()
    lw = {
        "attn_g": p["attn_norm_g"][l][None],
        "w_in": jnp.pad(p["w_in"][l], ((0, 0), (0, IN_COLS_PAD - IN_COLS))).astype(BF16),
        "conv_w": p["conv_w"][l],
        "gq": jnp.tile(p["q_norm_g"][l], N_HEADS)[None],
        "gk": jnp.tile(p["k_norm_g"][l], N_KV)[None],
        "giq": jnp.tile(p["iq_norm_g"][l], N_IDX_HEADS)[None],
        "gik": jnp.pad(p["ik_norm_g"][l], (0, LANES - IDX_DIM))[None],
        "mq": _block_diag_mean(ATTN_DIM, HEAD_DIM),
        "mk": _block_diag_mean(N_KV * HEAD_DIM, HEAD_DIM),
        "miq": _block_diag_mean(N_IDX_HEADS * IDX_DIM, IDX_DIM),
        "mik": _block_diag_mean(LANES, IDX_DIM, valid=IDX_DIM),
        "s1": s1, "s2": s2, "t1": t1, "t2": t2,
        "wo_a": p["w_out"][l][:CONV_DIM].astype(BF16),
        "wo_b": p["w_out"][l][CONV_DIM:].astype(BF16),
        "ffn_g": p["ffn_norm_g"][l][None],
    }
    j = l // 2
    if l % 2 == 0:
        lw.update(wg=p["ffn_w_gate"][j].astype(BF16), wu=p["ffn_w_up"][j].astype(BF16),
                  wd=p["ffn_w_down"][j].astype(BF16))
    else:
        r = jnp.pad(p["router_w"][j], ((0, 0), (0, LANES - N_EXPERTS)))
        r_hi = r.astype(BF16)
        lw.update(r_hi=r_hi, r_lo=(r - r_hi.astype(F32)).astype(BF16),
                  eg=p["moe_w_gate"][j].astype(BF16), eu=p["moe_w_up"][j].astype(BF16),
                  ed=p["moe_w_down"][j].astype(BF16))
    assert d == lw["wo_a"].shape[1]
    return lw


def _pick_tile(n, pref):
    t = min(n, pref)
    while n % t:
        t //= 2
    return t


def _mixer(x, prev, lw, *, tq, past_k2=None, past_v=None, past_ik=None, q_pos0=0):
    b, t, d = x.shape
    yc, q, k, v, k2, vb, iqcat, iw, ik, ikcat, cst = _inproj(x, prev, lw, _pick_tile(t, 1024))
    if past_k2 is not None:
        k2 = jnp.concatenate([past_k2, k2], axis=1)
        vb = jnp.concatenate([past_v, vb], axis=1)
        ikcat = jnp.concatenate([past_ik, ikcat], axis=1)
    n_keys = k2.shape[1]
    pad = (-n_keys) % KEY_CHUNK
    if pad:
        k2, vb, ikcat = (jnp.pad(a, ((0, 0), (0, pad), (0, 0))) for a in (k2, vb, ikcat))
    at = _attention(q, iqcat, iw, k2, vb, ikcat, tq=tq, n_keys=n_keys, q_pos0=q_pos0)
    m = b * t
    x2, yc2, at2 = x.reshape(m, d), yc.reshape(m, CONV_DIM), at.reshape(m, ATTN_DIM)
    if "wg" in lw:
        y = _ffn_dense(x2, yc2, at2, lw, _pick_tile(m, 512))
    else:
        y = _ffn_moe(x2, yc2, at2, lw, _pick_tile(m, 1024))
    return (y.reshape(b, t, d), k.reshape(b, t, N_KV, HEAD_DIM), v.reshape(b, t, N_KV, HEAD_DIM),
            ik, cst)


def kernel(x_prompt, x_sample, cache_k, cache_v, cache_ik, state_conv, attn_norm_g, w_in, conv_w,
           q_norm_g, k_norm_g, iq_norm_g, ik_norm_g, w_out, ffn_norm_g, ffn_w_gate, ffn_w_up,
           ffn_w_down, router_w, moe_w_gate, moe_w_up, moe_w_down):
    params = dict(attn_norm_g=attn_norm_g, w_in=w_in, conv_w=conv_w, q_norm_g=q_norm_g,
                  k_norm_g=k_norm_g, iq_norm_g=iq_norm_g, ik_norm_g=ik_norm_g, w_out=w_out,
                  ffn_norm_g=ffn_norm_g, ffn_w_gate=ffn_w_gate, ffn_w_up=ffn_w_up,
                  ffn_w_down=ffn_w_down, router_w=router_w, moe_w_gate=moe_w_gate,
                  moe_w_up=moe_w_up, moe_w_down=moe_w_down)
    depth = w_in.shape[0]
    xp, xs = x_prompt, x_sample
    bp = xp.shape[0]
    bs, ts, _ = xs.shape
    past = cache_k.shape[2]
    kvd = N_KV * HEAD_DIM
    outs = [[] for _ in range(8)]
    for l in range(depth):
        lw = _layer_weights(l, params)
        prev0 = jnp.zeros((bp, CONV_W - 1, CONV_DIM), F32)
        xp, kp, vp, ikp, cp = _mixer(xp, prev0, lw, tq=Q_BLOCK)

        ck = cache_k[l].reshape(bs, past, kvd)
        g0, g1 = ck[..., :HEAD_DIM], ck[..., HEAD_DIM:]
        past_k2 = jnp.concatenate([g0, g0, g1, g1], axis=-1).astype(BF16)
        past_v = cache_v[l].reshape(bs, past, kvd).astype(BF16)
        cik = cache_ik[l]
        cik_hi = cik.astype(BF16)
        cik_lo = (cik - cik_hi.astype(F32)).astype(BF16)
        past_ik = jnp.concatenate([cik_hi, cik_hi, cik_lo, jnp.zeros_like(cik_hi)], axis=-1)
        xs, ks, vs, iks, cs = _mixer(xs, state_conv[l], lw, tq=ts, past_k2=past_k2,
                                     past_v=past_v, past_ik=past_ik, q_pos0=past)
        for lst, val in zip(outs, (kp, vp, ikp, cp, ks, vs, iks, cs)):
            lst.append(val)
    kp, vp, ikp, cp, ks, vs, iks, cs = (jnp.stack(o) for o in outs)
    return (xp, xs, kp, vp, ikp, cp, ks, vs, iks, cs)
```

```python
import functools
import math

import numpy as np
import jax
import jax.numpy as jnp
from jax import lax
from jax.experimental import pallas as pl
from jax.experimental.pallas import tpu as pltpu

CHUNK = 64
CONV_DIM = 512
CONV_W = 3
N_HEADS = 8
HEAD_DIM = 64
ATTN_DIM = N_HEADS * HEAD_DIM
N_KV = 2
N_IDX_HEADS = 8
IDX_DIM = 32
TOPK_MAX = 256
Q_BLOCK = 128
N_EXPERTS = 8
EPS = 1e-6
COL_SIZES = (CONV_DIM, CONV_DIM, CONV_DIM, ATTN_DIM, N_KV * HEAD_DIM, N_KV * HEAD_DIM,
             N_IDX_HEADS * IDX_DIM, IDX_DIM, N_IDX_HEADS)
IN_COLS = sum(COL_SIZES)

LANES = 128
V7X_VMEM_LIMIT = 56 * 1024 * 1024

IN_COLS_PAD = ((IN_COLS + LANES - 1) // LANES) * LANES
TAIL_OFF = IN_COLS_PAD - LANES
IW_LANE = IDX_DIM
QSCALE = HEAD_DIM ** -0.5 * math.log2(math.e)
IWSCALE = IDX_DIM ** -0.5 * N_IDX_HEADS ** -0.5
NEG = -1e30
INT_MIN = -2 ** 31
INT_MAX = 2 ** 31 - 1
KEY_CHUNK = 512
BISECT_ROUND = 4
BISECT_FREE_PASSES = 8
EXTRACT_BELOW = 2
BISECT_MAX_ROUNDS = -(-(32 + 3) // BISECT_ROUND) + EXTRACT_BELOW
FIRST_PROBE_GAP = 3 << 23

F32 = jnp.float32
BF16 = jnp.bfloat16
I32 = jnp.int32


def _split_hi_lo(x):
    hi = x.astype(BF16)
    lo = (x - hi.astype(F32)).astype(BF16)
    return hi, lo


def _group_rsqrt(x, g_ref):
    hi, lo = _split_hi_lo(x * x)
    g = g_ref[...]
    ms = jnp.dot(hi, g, preferred_element_type=F32) + jnp.dot(lo, g, preferred_element_type=F32)
    return lax.rsqrt(ms + EPS)


def _rmsnorm(x, g):
    return x * lax.rsqrt(jnp.mean(x * x, axis=-1, keepdims=True) + EPS) * g


def _inproj_kernel(x_ref, prev_ref, g_ref, w_ref, cw_ref, gq_ref, gk_ref, giq_ref, gik_ref,
                   mq_ref, mk_ref, miq_ref, mik_ref, s1_ref, s2_ref, t1_ref, t2_ref,
                   yconv_ref, q_ref, k_ref, v_ref, k2_ref, vb_ref, iqcat_ref, iw_ref, ik_ref,
                   ikcat_ref, cst_ref, carry_ref):
    t = pl.program_id(1)
    nt = pl.num_programs(1)
    x = x_ref[0]
    tm = x.shape[0]
    h = _rmsnorm(x, g_ref[...]).astype(BF16)
    p = jnp.dot(h, w_ref[...], preferred_element_type=F32)
    o = np.cumsum((0,) + COL_SIZES)
    c_gate, b_gate, u = p[:, o[0]:o[1]], p[:, o[1]:o[2]], p[:, o[2]:o[3]]
    q, k, v, iq = p[:, o[3]:o[4]], p[:, o[4]:o[5]], p[:, o[5]:o[6]], p[:, o[6]:o[7]]
    tail = p[:, TAIL_OFF:]

    ci = c_gate * u

    @pl.when(t == 0)
    def _():
        carry_ref[8 - (CONV_W - 1):, :] = prev_ref[0]

    row = lax.broadcasted_iota(I32, ci.shape, 0)
    c7 = carry_ref[7:8, :]
    c6 = carry_ref[6:7, :]
    s1 = jnp.where(row == 0, c7, pltpu.roll(ci, 1, axis=0))
    s2 = jnp.where(row == 0, c6, jnp.where(row == 1, c7, pltpu.roll(ci, 2, axis=0)))
    cw = cw_ref[...]
    conv = cw[0:1] * s2 + cw[1:2] * s1 + cw[2:3] * ci
    yconv_ref[0] = (b_gate * conv).astype(BF16)
    carry_ref[...] = ci[tm - 8:, :]

    @pl.when(t == nt - 1)
    def _():
        cst_ref[0] = ci[tm - (CONV_W - 1):, :]

    qn = q * _group_rsqrt(q, mq_ref) * gq_ref[...]
    q_ref[0] = (qn * QSCALE).astype(BF16)
    kn = k * _group_rsqrt(k, mk_ref) * gk_ref[...]
    k_ref[0] = kn
    v_ref[0] = v
    lane = lax.broadcasted_iota(I32, kn.shape, 1)
    kr = pltpu.roll(kn, HEAD_DIM, axis=1)
    k2_ref[0, :, 0:LANES] = jnp.where(lane < HEAD_DIM, kn, kr).astype(BF16)
    k2_ref[0, :, LANES:2 * LANES] = jnp.where(lane < HEAD_DIM, kr, kn).astype(BF16)
    vb_ref[0] = v.astype(BF16)

    iqn = iq * _group_rsqrt(iq, miq_ref) * giq_ref[...]
    ihi, ilo = _split_hi_lo(iqn)
    iqcat_ref[0] = (jnp.dot(ihi, s1_ref[...], preferred_element_type=F32)
                    + jnp.dot(ilo, s2_ref[...], preferred_element_type=F32)).astype(BF16)
    tn = tail * _group_rsqrt(tail, mik_ref) * gik_ref[...]
    ik_ref[0] = tn[:, 0:IDX_DIM]
    thi, tlo = _split_hi_lo(tn)
    ikcat_ref[0] = (jnp.dot(thi, t1_ref[...], preferred_element_type=F32)
                    + jnp.dot(tlo, t2_ref[...], preferred_element_type=F32)).astype(BF16)
    iw_ref[0] = tail * IWSCALE


def _const_spec(shape):
    nd = len(shape)
    return pl.BlockSpec(shape, lambda *_: (0,) * nd)


def _inproj(x, prev, lw, tm):
    b, t, d = x.shape
    grid = (b, t // tm)
    tile = lambda n: pl.BlockSpec((1, tm, n), lambda i, j: (i, j, 0))
    consts = [lw["attn_g"], lw["w_in"], lw["conv_w"], lw["gq"], lw["gk"], lw["giq"], lw["gik"],
              lw["mq"], lw["mk"], lw["miq"], lw["mik"], lw["s1"], lw["s2"], lw["t1"], lw["t2"]]
    in_specs = ([tile(d), pl.BlockSpec((1, CONV_W - 1, CONV_DIM), lambda i, j: (i, 0, 0))]
                + [_const_spec(c.shape) for c in consts])
    kvd = N_KV * HEAD_DIM
    out_shape = (
        jax.ShapeDtypeStruct((b, t, CONV_DIM), BF16),
        jax.ShapeDtypeStruct((b, t, ATTN_DIM), BF16),
        jax.ShapeDtypeStruct((b, t, kvd), F32),
        jax.ShapeDtypeStruct((b, t, kvd), F32),
        jax.ShapeDtypeStruct((b, t, 2 * kvd), BF16),
        jax.ShapeDtypeStruct((b, t, kvd), BF16),
        jax.ShapeDtypeStruct((b, t, N_IDX_HEADS * LANES), BF16),
        jax.ShapeDtypeStruct((b, t, LANES), F32),
        jax.ShapeDtypeStruct((b, t, IDX_DIM), F32),
        jax.ShapeDtypeStruct((b, t, LANES), BF16),
        jax.ShapeDtypeStruct((b, CONV_W - 1, CONV_DIM), F32),
    )
    out_specs = (tile(CONV_DIM), tile(ATTN_DIM), tile(kvd), tile(kvd), tile(2 * kvd), tile(kvd),
                 tile(N_IDX_HEADS * LANES), tile(LANES), tile(IDX_DIM), tile(LANES),
                 pl.BlockSpec((1, CONV_W - 1, CONV_DIM), lambda i, j: (i, 0, 0)))
    return pl.pallas_call(
        _inproj_kernel, grid=grid, in_specs=in_specs, out_specs=out_specs, out_shape=out_shape,
        scratch_shapes=[pltpu.VMEM((8, CONV_DIM), F32)],
        compiler_params=pltpu.CompilerParams(dimension_semantics=("arbitrary", "arbitrary"),
                                             vmem_limit_bytes=V7X_VMEM_LIMIT),
        name="inproj",
    )(x, prev, *consts)


def _slabs(n):
    return [slice(j * LANES, (j + 1) * LANES) for j in range(n // LANES)]


def _tree(op, xs):
    while len(xs) > 1:
        xs = [op(xs[i], xs[i + 1]) for i in range(0, len(xs) - 1, 2)] + ([xs[-1]] if len(xs) % 2 else [])
    return xs[0]


def _attn_kernel(q_ref, iqcat_ref, iw_ref, k2_ref, v_ref, ikcat_ref, tri_ref, o_ref,
                 keyst_ref, keys_ref, lga_ref, lgb_ref, lhs_ref, wb_ref, qg_ref, sa_ref, sb_ref,
                 mxa_ref, mxb_ref, m_ref, l_ref, acc_ref,
                 *, tq, kc, n_keys, q_pos0, topk):
    qi = pl.program_id(1)
    q0 = q_pos0 + qi * tq
    shift = CHUNK.bit_length() - 1

    def key_limit(pos):
        return jnp.minimum(((pos >> shift) + 1) << shift, n_keys)

    lane = lax.broadcasted_iota(I32, (tq, LANES), 1)
    lim = key_limit(q0 + lax.broadcasted_iota(I32, (tq, LANES), 0))
    qlane = lax.broadcasted_iota(I32, (1, LANES), 1)
    lim_t = jnp.where(qlane < tq, key_limit(q0 + qlane), 0)
    n_chunks = (key_limit(q0 + tq - 1) + kc - 1) // kc
    kfl = jnp.minimum(lim_t, topk).astype(F32)

    iqc = iqcat_ref[0]
    for h in range(N_IDX_HEADS):
        lhs_ref[h * tq:(h + 1) * tq, :] = iqc[:, h * LANES:(h + 1) * LANES]
    iw = iw_ref[0]
    for h in range(N_IDX_HEADS):
        col = jnp.sum(jnp.where(lane == IW_LANE + h, iw, 0.0), axis=1, keepdims=True)
        wb_ref[h] = jnp.broadcast_to(col, (tq, LANES))
    pad_rows = jnp.full((LANES - tq, LANES), INT_MIN, I32) if tq < LANES else None

    last_chunk = keys_ref.shape[0] - 1

    def logits(c, lg_ref):
        off = pl.multiple_of(jnp.minimum(c, last_chunk) * kc, kc)
        lg_ref[...] = lax.dot_general(lhs_ref[...], ikcat_ref[0, pl.ds(off, kc), :],
                                      (((1,), (1,)), ((), ())), preferred_element_type=F32)

    def keys_from(c, lg_ref, kmax):
        for j, sl in enumerate(_slabs(kc)):
            sc = None
            for h in range(N_IDX_HEADS):
                term = jnp.maximum(lg_ref[h * tq:(h + 1) * tq, sl], 0.0) * wb_ref[h]
                sc = term if sc is None else sc + term
            sc = sc + 0.0
            bits = lax.bitcast_convert_type(sc, I32)
            key = bits ^ ((bits >> 31) & INT_MAX)
            key = jnp.where(c * kc + j * LANES + lane < lim, key, INT_MIN)
            keys_ref[c, :, sl] = key
            if pad_rows is not None:
                key = jnp.concatenate([key, pad_rows], axis=0)
            kt = key.T
            keyst_ref[c, sl, :] = kt
            kmax = jnp.maximum(kmax, _tree(jnp.maximum, [kt[8 * i:8 * i + 8, :]
                                                         for i in range(LANES // 8)]))
        return kmax

    logits(0, lga_ref)

    def score_pair(i, kmax):
        logits(2 * i + 1, lgb_ref)
        kmax = keys_from(2 * i, lga_ref, kmax)
        logits(2 * i + 2, lga_ref)
        return keys_from(2 * i + 1, lgb_ref, kmax)

    odd_chunks = n_chunks % 2 == 1
    kmax = lax.fori_loop(0, n_chunks // 2, score_pair, jnp.full((8, LANES), INT_MIN, I32))
    kmax = lax.cond(odd_chunks, lambda k: keys_from(n_chunks - 1, lga_ref, k), lambda k: k, kmax)
    kmax = jnp.max(kmax, axis=0, keepdims=True)

    tree_sum = functools.partial(_tree, jnp.add)

    def count_ge(thr):
        def body(c, acc):
            hit = jnp.where(keyst_ref[c] >= thr, 1.0, 0.0)
            return acc + tree_sum([hit[8 * i:8 * i + 8, :] for i in range(kc // 8)])

        part = lax.fori_loop(0, n_chunks, body, jnp.zeros((8, LANES), F32))
        return jnp.sum(part, axis=0, keepdims=True)

    lo0 = jnp.full((1, LANES), INT_MIN + 1, I32)

    def next_probe(lo, hi):
        mid = (lo & hi) + ((lo ^ hi) >> 1)
        floor = lo == lo0
        return jnp.where(jnp.logical_and(floor, hi > 1), 1,
                         jnp.where(jnp.logical_and(floor, hi == 1), 0, mid))

    def bisect_pass(lo, hi, clo, chi, mid):
        cnt = count_ge(mid)
        ge = cnt >= kfl
        exact = cnt == kfl
        lo_n = jnp.where(ge, mid, lo)
        hi_n = jnp.where(exact, mid + 1, jnp.where(ge, hi, mid))
        return lo_n, hi_n, jnp.where(ge, cnt, clo), jnp.where(ge, chi, cnt)

    def max_below(bound):
        def body(c, acc):
            kch = keyst_ref[c]
            cand = jnp.where(kch < bound, kch, INT_MIN)
            return jnp.maximum(acc, _tree(jnp.maximum, [cand[8 * i:8 * i + 8, :] for i in range(kc // 8)]))

        part = lax.fori_loop(0, n_chunks, body, jnp.full((8, LANES), INT_MIN, I32))
        return jnp.max(part, axis=0, keepdims=True)

    def bisect_round(state):
        for _ in range(BISECT_ROUND):
            state = bisect_pass(*state, next_probe(state[0], state[1]))
        return state

    def extract_step(state):
        lo, hi, clo, chi = state
        top = max_below(hi)
        cnt = count_ge(top)
        open_ = hi != lo + 1
        done = jnp.logical_and(open_, cnt >= kfl)
        down = jnp.logical_and(open_, cnt < kfl)
        return (jnp.where(done, top, lo), jnp.where(done, top + 1, jnp.where(down, top, hi)),
                jnp.where(done, cnt, clo), jnp.where(down, cnt, chi))

    def open_need(lo, hi, chi):
        need = jnp.where(hi != lo + 1, kfl - chi, 0.0)
        return jnp.max(need, axis=1, keepdims=True)[0, 0]

    def bisect_cond(carry):
        it, need = carry[0], carry[1]
        return jnp.logical_and(need > 0.0, it < BISECT_MAX_ROUNDS)

    def bisect_body(carry):
        it, need = carry[0], carry[1]
        state = lax.cond(need <= EXTRACT_BELOW, extract_step, bisect_round, tuple(carry[2:]))
        return (it + 1, open_need(state[0], state[1], state[3])) + tuple(state)

    hi = jnp.maximum(jnp.minimum(kmax, INT_MAX - 1) + 1, lo0 + 1)
    first = jnp.where(kmax > INT_MIN + 2 + FIRST_PROBE_GAP, kmax - FIRST_PROBE_GAP, next_probe(lo0, hi))
    state = bisect_pass(lo0, hi, lim_t.astype(F32), jnp.zeros((1, LANES), F32), first)
    for _ in range(BISECT_FREE_PASSES):
        state = bisect_pass(*state, next_probe(state[0], state[1]))
    _, _, thr, _, clo, chi = lax.while_loop(
        bisect_cond, bisect_body, (jnp.int32(0), open_need(state[0], state[1], state[3])) + tuple(state))

    extra = clo - kfl
    any_tie = jnp.max(extra, axis=1, keepdims=True)[0, 0]

    @pl.when(any_tie > 0.0)
    def _():
        need = jnp.where(extra > 0.0, kfl - chi, jnp.inf)

        def demote(c, seen):
            kch = keyst_ref[c]
            tied = kch == thr
            rank = seen + jnp.dot(tri_ref[...], jnp.where(tied, 1.0, 0.0).astype(BF16),
                                  preferred_element_type=F32)
            kch = jnp.where(jnp.logical_and(tied, rank > need), kch - 1, kch)
            keyst_ref[c] = kch
            for sl in _slabs(kc):
                keys_ref[c, :, sl] = kch[sl, :].T[:tq, :]
            return rank[kc - 1:kc, :]

        lax.fori_loop(0, n_chunks, demote, jnp.zeros((1, LANES), F32))

    thr_rows = jnp.broadcast_to(thr, (LANES, LANES)).T[:tq, :]
    qv = q_ref[0].astype(F32)
    heads_per_group = N_HEADS // N_KV
    rows = heads_per_group * tq
    for g in range(N_KV):
        for r in range(heads_per_group):
            hd = g * heads_per_group + r
            slab = qv[:, (hd // 2) * LANES:(hd // 2 + 1) * LANES]
            keep = (lane < HEAD_DIM) if hd % 2 == 0 else (lane >= HEAD_DIM)
            qg_ref[g, r * tq:(r + 1) * tq, :] = jnp.where(keep, slab, 0.0).astype(BF16)
    m_ref[...] = jnp.full(m_ref.shape, NEG, F32)
    l_ref[...] = jnp.zeros(l_ref.shape, F32)
    acc_ref[...] = jnp.zeros(acc_ref.shape, F32)

    def qk_scores(c, s_ref, mx_ref):
        c = jnp.minimum(c, last_chunk)
        off = pl.multiple_of(c * kc, kc)
        bias = [jnp.where(keys_ref[c, :, sl] >= thr_rows, 0.0, NEG) for sl in _slabs(kc)]
        for g in range(N_KV):
            kch = k2_ref[0, pl.ds(off, kc), g * LANES:(g + 1) * LANES]
            s = lax.dot_general(qg_ref[g], kch, (((1,), (1,)), ((), ())),
                                preferred_element_type=F32)
            for r in range(heads_per_group):
                rs = slice(r * tq, (r + 1) * tq)
                masked = [s[rs, sl] + b for sl, b in zip(_slabs(kc), bias)]
                for sl, piece in zip(_slabs(kc), masked):
                    s_ref[g, rs, sl] = piece
                part = _tree(jnp.maximum, masked)
                mx_ref[g, rs, :] = jnp.broadcast_to(jnp.max(part, axis=1, keepdims=True), (tq, LANES))

    def softmax_pv(c, s_ref, mx_ref):
        off = pl.multiple_of(c * kc, kc)
        vch = v_ref[0, pl.ds(off, kc), :]
        for g in range(N_KV):
            m_prev = m_ref[g]
            m_new = jnp.maximum(m_prev, mx_ref[g])
            alpha = jnp.exp2(m_prev - m_new)
            probs = []
            for r in range(heads_per_group):
                rs = slice(r * tq, (r + 1) * tq)
                ps = [jnp.exp2(s_ref[g, rs, sl] - m_new[rs]) for sl in _slabs(kc)]
                probs.append(jnp.concatenate(ps, axis=1).astype(BF16))
                l_ref[g, rs, :] = alpha[rs] * l_ref[g, rs, :] + tree_sum(ps)
            acc_ref[g] = alpha * acc_ref[g] + jnp.dot(jnp.concatenate(probs, axis=0), vch,
                                                      preferred_element_type=F32)
            m_ref[g] = m_new

    stage_a = (sa_ref, mxa_ref)
    stage_b = (sb_ref, mxb_ref)
    qk_scores(0, *stage_a)

    def attend_pair(i, carry):
        qk_scores(2 * i + 1, *stage_b)
        softmax_pv(2 * i, *stage_a)
        qk_scores(2 * i + 2, *stage_a)
        softmax_pv(2 * i + 1, *stage_b)
        return carry

    lax.fori_loop(0, n_chunks // 2, attend_pair, 0)

    @pl.when(odd_chunks)
    def _():
        softmax_pv(n_chunks - 1, *stage_a)

    for g in range(N_KV):
        out = acc_ref[g] / jnp.sum(l_ref[g], axis=1, keepdims=True)
        for jj in range(heads_per_group // 2):
            a = out[(2 * jj) * tq:(2 * jj + 1) * tq, :]
            bb = out[(2 * jj + 1) * tq:(2 * jj + 2) * tq, :]
            if g % 2 == 0:
                slab = jnp.where(lane < HEAD_DIM, a, pltpu.roll(bb, HEAD_DIM, axis=1))
            else:
                slab = jnp.where(lane < HEAD_DIM, pltpu.roll(a, HEAD_DIM, axis=1), bb)
            so = (g * heads_per_group // 2 + jj) * LANES
            o_ref[0, :, so:so + LANES] = slab.astype(BF16)


def _attention(q, iqcat, iw, k2, vb, ikcat, *, tq, n_keys, q_pos0):
    b, t, _ = q.shape
    lpad = k2.shape[1]
    kc = KEY_CHUNK
    assert lpad % kc == 0 and t % tq == 0 and tq <= LANES and CHUNK & (CHUNK - 1) == 0
    topk = min(TOPK_MAX, n_keys // 4)
    kern = functools.partial(_attn_kernel, tq=tq, kc=kc, n_keys=n_keys, q_pos0=q_pos0, topk=topk)
    qtile = lambda n: pl.BlockSpec((1, tq, n), lambda i, j: (i, j, 0))
    ktile = lambda n: pl.BlockSpec((1, lpad, n), lambda i, j: (i, 0, 0))
    rows = (N_HEADS // N_KV) * tq
    return pl.pallas_call(
        kern, grid=(b, t // tq),
        in_specs=[qtile(ATTN_DIM), qtile(N_IDX_HEADS * LANES), qtile(LANES),
                  ktile(2 * N_KV * HEAD_DIM), ktile(N_KV * HEAD_DIM), ktile(LANES),
                  pl.BlockSpec((kc, kc), lambda i, j: (0, 0))],
        out_specs=qtile(ATTN_DIM),
        out_shape=jax.ShapeDtypeStruct((b, t, ATTN_DIM), BF16),
        scratch_shapes=[
            pltpu.VMEM((lpad // kc, kc, LANES), I32),
            pltpu.VMEM((lpad // kc, tq, kc), I32),
            pltpu.VMEM((N_IDX_HEADS * tq, kc), F32),
            pltpu.VMEM((N_IDX_HEADS * tq, kc), F32),
            pltpu.VMEM((N_IDX_HEADS * tq, LANES), BF16),
            pltpu.VMEM((N_IDX_HEADS, tq, LANES), F32),
            pltpu.VMEM((N_KV, rows, LANES), BF16),
            pltpu.VMEM((N_KV, rows, kc), F32),
            pltpu.VMEM((N_KV, rows, kc), F32),
            pltpu.VMEM((N_KV, rows, LANES), F32),
            pltpu.VMEM((N_KV, rows, LANES), F32),
            pltpu.VMEM((N_KV, rows, LANES), F32),
            pltpu.VMEM((N_KV, rows, LANES), F32),
            pltpu.VMEM((N_KV, rows, LANES), F32),
        ],
        compiler_params=pltpu.CompilerParams(dimension_semantics=("arbitrary", "arbitrary"),
                                             vmem_limit_bytes=V7X_VMEM_LIMIT),
        name="attention",
    )(q, iqcat, iw, k2, vb, ikcat, jnp.asarray(np.tril(np.ones((kc, kc), np.float32)), BF16))


def _mix_in(x_ref, yc_ref, at_ref, woa_ref, wob_ref, g_ref):
    x1 = (x_ref[...] + jnp.dot(yc_ref[...], woa_ref[...], preferred_element_type=F32)
          + jnp.dot(at_ref[...], wob_ref[...], preferred_element_type=F32))
    return x1, _rmsnorm(x1, g_ref[...])


def _swiglu(hb, wg, wu, wd):
    a = jnp.dot(hb, wg, preferred_element_type=F32)
    b = jnp.dot(hb, wu, preferred_element_type=F32)
    act = (a * (1.0 / (1.0 + jnp.exp(-a))) * b).astype(BF16)
    return jnp.dot(act, wd, preferred_element_type=F32)


def _ffn_dense_kernel(x_ref, yc_ref, at_ref, woa_ref, wob_ref, g_ref, wg_ref, wu_ref, wd_ref, o_ref):
    x1, hn = _mix_in(x_ref, yc_ref, at_ref, woa_ref, wob_ref, g_ref)
    o_ref[...] = x1 + _swiglu(hn.astype(BF16), wg_ref[...], wu_ref[...], wd_ref[...])


def _ffn_dense(x, yc, at, lw, tm):
    m, d = x.shape
    row = lambda n: pl.BlockSpec((tm, n), lambda i: (i, 0))
    consts = [lw["wo_a"], lw["wo_b"], lw["ffn_g"], lw["wg"], lw["wu"], lw["wd"]]
    return pl.pallas_call(
        _ffn_dense_kernel, grid=(m // tm,),
        in_specs=[row(d), row(CONV_DIM), row(ATTN_DIM)]
        + [pl.BlockSpec(c.shape, lambda i: (0, 0), pipeline_mode=pl.Buffered(1)) for c in consts],
        out_specs=row(d), out_shape=jax.ShapeDtypeStruct((m, d), F32),
        compiler_params=pltpu.CompilerParams(dimension_semantics=("arbitrary",),
                                             vmem_limit_bytes=V7X_VMEM_LIMIT),
        name="ffn_dense",
    )(x, yc, at, *consts)


def _ffn_moe_kernel(x_ref, yc_ref, at_ref, woa_ref, wob_ref, g_ref, rhi_ref, rlo_ref, tril_ref,
                    eg_ref, eu_ref, ed_ref, o_ref, h_ref, gate_ref, rank_ref, rankt_ref, cnt_ref, acc_ref,
                    *, cap):
    e = pl.program_id(1)
    tm = x_ref.shape[0]
    lane = lax.broadcasted_iota(I32, (tm, LANES), 1)

    @pl.when(e == 0)
    def _():
        x1, hn = _mix_in(x_ref, yc_ref, at_ref, woa_ref, wob_ref, g_ref)
        hi, lo = _split_hi_lo(hn)
        h_ref[...] = hi
        rhi = rhi_ref[...]
        logits = (jnp.dot(hi, rhi, preferred_element_type=F32)
                  + jnp.dot(lo, rhi, preferred_element_type=F32)
                  + jnp.dot(hi, rlo_ref[...], preferred_element_type=F32))
        logits = jnp.where(lane < N_EXPERTS, logits, -jnp.inf)
        ex = jnp.exp(logits - jnp.max(logits, axis=1, keepdims=True))
        probs = ex / jnp.sum(ex, axis=1, keepdims=True)
        p1 = jnp.max(probs, axis=1, keepdims=True)
        i1 = jnp.min(jnp.where(probs == p1, lane, LANES), axis=1, keepdims=True)
        rest = jnp.where(lane == i1, -1.0, probs)
        p2 = jnp.max(rest, axis=1, keepdims=True)
        i2 = jnp.min(jnp.where(rest == p2, lane, LANES), axis=1, keepdims=True)
        den = p1 + p2
        gate_ref[...] = jnp.where(lane == i1, p1 / den, jnp.where(lane == i2, p2 / den, 0.0))
        routed = jnp.where(jnp.logical_or(lane == i1, lane == i2), 1.0, 0.0)
        before = jnp.dot(tril_ref[...], routed.astype(BF16), preferred_element_type=F32)
        rank = jnp.where(routed > 0.0, before, -1.0)
        rank_ref[...] = rank
        for r0 in range(0, tm, LANES):
            nr = min(LANES, tm - r0)
            blk = rank[r0:r0 + nr, :]
            if nr < LANES:
                blk = jnp.concatenate([blk, jnp.full((LANES - nr, LANES), -1.0, F32)], axis=0)
            rt = blk.T
            for ex_id in range(N_EXPERTS):
                rankt_ref[ex_id, :, r0:r0 + nr] = rt[ex_id:ex_id + 1, :nr]
        cnt_ref[...] = jnp.broadcast_to(jnp.sum(routed, axis=0, keepdims=True), cnt_ref.shape)
        acc_ref[...] = x1

    mine = lane == e
    rank_col = jnp.sum(jnp.where(mine, rank_ref[...], 0.0), axis=1, keepdims=True)
    gate_col = jnp.sum(jnp.where(mine, gate_ref[...], 0.0), axis=1, keepdims=True)
    rank_row = rankt_ref[e]
    n_tok = jnp.sum(jnp.where(mine[:8], cnt_ref[...], 0.0), axis=1, keepdims=True)[0, 0]
    n_sub = (n_tok.astype(I32) + cap - 1) // cap
    kpad = -(-cap // LANES) * LANES
    slot_r = lax.broadcasted_iota(I32, (cap, tm), 0).astype(F32)
    slot_c = lax.broadcasted_iota(I32, (tm, kpad), 1).astype(F32)

    def sub_block(sb, carry):
        base = (sb * cap).astype(F32)
        gather = jnp.where(rank_row - base == slot_r, 1.0, 0.0).astype(BF16)
        hs = jnp.dot(gather, h_ref[...], preferred_element_type=F32).astype(BF16)
        y = _swiglu(hs, eg_ref[0], eu_ref[0], ed_ref[0]).astype(BF16)
        if kpad > cap:
            y = jnp.concatenate([y, jnp.zeros((kpad - cap, y.shape[1]), BF16)], axis=0)
        scatter = jnp.where(rank_col - base == slot_c, 1.0, 0.0).astype(BF16)
        acc_ref[...] += gate_col * jnp.dot(scatter, y, preferred_element_type=F32)
        return carry

    lax.fori_loop(0, n_sub, sub_block, 0)

    @pl.when(e == pl.num_programs(1) - 1)
    def _():
        o_ref[...] = acc_ref[...]


def _ffn_moe(x, yc, at, lw, tm):
    m, d = x.shape
    ne, _, fe = lw["eg"].shape
    cap = min(tm, -(-(9 * tm // 32) // 16) * 16)
    row = lambda n: pl.BlockSpec((tm, n), lambda i, e: (i, 0))
    tril = jnp.asarray(np.tril(np.ones((tm, tm), np.float32), -1), BF16)
    consts = [lw["wo_a"], lw["wo_b"], lw["ffn_g"], lw["r_hi"], lw["r_lo"], tril]
    return pl.pallas_call(
        functools.partial(_ffn_moe_kernel, cap=cap), grid=(m // tm, ne),
        in_specs=[row(d), row(CONV_DIM), row(ATTN_DIM)]
        + [pl.BlockSpec(c.shape, lambda i, e: (0, 0), pipeline_mode=pl.Buffered(1)) for c in consts]
        + [pl.BlockSpec((1, d, fe), lambda i, e: (e, 0, 0)),
           pl.BlockSpec((1, d, fe), lambda i, e: (e, 0, 0)),
           pl.BlockSpec((1, fe, d), lambda i, e: (e, 0, 0))],
        out_specs=row(d), out_shape=jax.ShapeDtypeStruct((m, d), F32),
        scratch_shapes=[pltpu.VMEM((tm, d), BF16),
                        pltpu.VMEM((tm, LANES), F32),
                        pltpu.VMEM((tm, LANES), F32),
                        pltpu.VMEM((ne, 1, tm), F32),
                        pltpu.VMEM((8, LANES), F32),
                        pltpu.VMEM((tm, d), F32)],
        compiler_params=pltpu.CompilerParams(dimension_semantics=("arbitrary", "arbitrary"),
                                             vmem_limit_bytes=V7X_VMEM_LIMIT),
        name="ffn_moe",
    )(x, yc, at, *consts, lw["eg"], lw["eu"], lw["ed"])


def _block_diag_mean(n, group, valid=None):
    m = np.zeros((n, n), np.float32)
    for s in range(0, n if valid is None else valid, group):
        m[s:s + group, s:s + group] = 1.0 / group
    return jnp.asarray(m, BF16)


def _placement_matrices():
    s1 = np.zeros((N_IDX_HEADS * IDX_DIM, N_IDX_HEADS * LANES), np.float32)
    s2 = np.zeros_like(s1)
    t1 = np.zeros((LANES, LANES), np.float32)
    t2 = np.zeros_like(t1)
    j = np.arange(IDX_DIM)
    for h in range(N_IDX_HEADS):
        s1[h * IDX_DIM + j, h * LANES + j] = 1.0
        s2[h * IDX_DIM + j, h * LANES + IDX_DIM + j] = 1.0
        s1[h * IDX_DIM + j, h * LANES + 2 * IDX_DIM + j] = 1.0
    t1[j, j] = 1.0
    t1[j, IDX_DIM + j] = 1.0
    t2[j, 2 * IDX_DIM + j] = 1.0
    return tuple(jnp.asarray(a, BF16) for a in (s1, s2, t1, t2))


def _layer_weights(l, p):
    d = p["w_in"].shape[1]
    s1, s2, t1, t2 = _placement_matrices()
    lw = {
        "attn_g": p["attn_norm_g"][l][None],
        "w_in": jnp.pad(p["w_in"][l], ((0, 0), (0, IN_COLS_PAD - IN_COLS))).astype(BF16),
        "conv_w": p["conv_w"][l],
        "gq": jnp.tile(p["q_norm_g"][l], N_HEADS)[None],
        "gk": jnp.tile(p["k_norm_g"][l], N_KV)[None],
        "giq": jnp.tile(p["iq_norm_g"][l], N_IDX_HEADS)[None],
        "gik": jnp.pad(p["ik_norm_g"][l], (0, LANES - IDX_DIM))[None],
        "mq": _block_diag_mean(ATTN_DIM, HEAD_DIM),
        "mk": _block_diag_mean(N_KV * HEAD_DIM, HEAD_DIM),
        "miq": _block_diag_mean(N_IDX_HEADS * IDX_DIM, IDX_DIM),
        "mik": _block_diag_mean(LANES, IDX_DIM, valid=IDX_DIM),
        "s1": s1, "s2": s2, "t1": t1, "t2": t2,
        "wo_a": p["w_out"][l][:CONV_DIM].astype(BF16),
        "wo_b": p["w_out"][l][CONV_DIM:].astype(BF16),
        "ffn_g": p["ffn_norm_g"][l][None],
    }
    j = l // 2
    if l % 2 == 0:
        lw.update(wg=p["ffn_w_gate"][j].astype(BF16), wu=p["ffn_w_up"][j].astype(BF16),
                  wd=p["ffn_w_down"][j].astype(BF16))
    else:
        r = jnp.pad(p["router_w"][j], ((0, 0), (0, LANES - N_EXPERTS)))
        r_hi = r.astype(BF16)
        lw.update(r_hi=r_hi, r_lo=(r - r_hi.astype(F32)).astype(BF16),
                  eg=p["moe_w_gate"][j].astype(BF16), eu=p["moe_w_up"][j].astype(BF16),
                  ed=p["moe_w_down"][j].astype(BF16))
    assert d == lw["wo_a"].shape[1]
    return lw


def _pick_tile(n, pref):
    t = min(n, pref)
    while n % t:
        t //= 2
    return t


def _mixer(x, prev, lw, *, tq, past_k2=None, past_v=None, past_ik=None, q_pos0=0):
    b, t, d = x.shape
    yc, q, k, v, k2, vb, iqcat, iw, ik, ikcat, cst = _inproj(x, prev, lw, _pick_tile(t, 1024))
    if past_k2 is not None:
        k2 = jnp.concatenate([past_k2, k2], axis=1)
        vb = jnp.concatenate([past_v, vb], axis=1)
        ikcat = jnp.concatenate([past_ik, ikcat], axis=1)
    n_keys = k2.shape[1]
    pad = (-n_keys) % KEY_CHUNK
    if pad:
        k2, vb, ikcat = (jnp.pad(a, ((0, 0), (0, pad), (0, 0))) for a in (k2, vb, ikcat))
    at = _attention(q, iqcat, iw, k2, vb, ikcat, tq=tq, n_keys=n_keys, q_pos0=q_pos0)
    m = b * t
    x2, yc2, at2 = x.reshape(m, d), yc.reshape(m, CONV_DIM), at.reshape(m, ATTN_DIM)
    if "wg" in lw:
        y = _ffn_dense(x2, yc2, at2, lw, _pick_tile(m, 512))
    else:
        y = _ffn_moe(x2, yc2, at2, lw, _pick_tile(m, 1024))
    return (y.reshape(b, t, d), k.reshape(b, t, N_KV, HEAD_DIM), v.reshape(b, t, N_KV, HEAD_DIM),
            ik, cst)


def kernel(x_prompt, x_sample, cache_k, cache_v, cache_ik, state_conv, attn_norm_g, w_in, conv_w,
           q_norm_g, k_norm_g, iq_norm_g, ik_norm_g, w_out, ffn_norm_g, ffn_w_gate, ffn_w_up,
           ffn_w_down, router_w, moe_w_gate, moe_w_up, moe_w_down):
    params = dict(attn_norm_g=attn_norm_g, w_in=w_in, conv_w=conv_w, q_norm_g=q_norm_g,
                  k_norm_g=k_norm_g, iq_norm_g=iq_norm_g, ik_norm_g=ik_norm_g, w_out=w_out,
                  ffn_norm_g=ffn_norm_g, ffn_w_gate=ffn_w_gate, ffn_w_up=ffn_w_up,
                  ffn_w_down=ffn_w_down, router_w=router_w, moe_w_gate=moe_w_gate,
                  moe_w_up=moe_w_up, moe_w_down=moe_w_down)
    depth = w_in.shape[0]
    xp, xs = x_prompt, x_sample
    bp = xp.shape[0]
    bs, ts, _ = xs.shape
    past = cache_k.shape[2]
    kvd = N_KV * HEAD_DIM
    outs = [[] for _ in range(8)]
    for l in range(depth):
        lw = _layer_weights(l, params)
        prev0 = jnp.zeros((bp, CONV_W - 1, CONV_DIM), F32)
        xp, kp, vp, ikp, cp = _mixer(xp, prev0, lw, tq=Q_BLOCK)

        ck = cache_k[l].reshape(bs, past, kvd)
        g0, g1 = ck[..., :HEAD_DIM], ck[..., HEAD_DIM:]
        past_k2 = jnp.concatenate([g0, g0, g1, g1], axis=-1).astype(BF16)
        past_v = cache_v[l].reshape(bs, past, kvd).astype(BF16)
        cik = cache_ik[l]
        cik_hi = cik.astype(BF16)
        cik_lo = (cik - cik_hi.astype(F32)).astype(BF16)
        past_ik = jnp.concatenate([cik_hi, cik_hi, cik_lo, jnp.zeros_like(cik_hi)], axis=-1)
        xs, ks, vs, iks, cs = _mixer(xs, state_conv[l], lw, tq=ts, past_k2=past_k2,
                                     past_v=past_v, past_ik=past_ik, q_pos0=past)
        for lst, val in zip(outs, (kp, vp, ikp, cp, ks, vs, iks, cs)):
            lst.append(val)
    kp, vp, ikp, cp, ks, vs, iks, cs = (jnp.stack(o) for o in outs)
    return (xp, xs, kp, vp, ikp, cp, ks, vs, iks, cs)
```

```python
import functools
import math

import numpy as np
import jax
import jax.numpy as jnp
from jax import lax
from jax.experimental import pallas as pl
from jax.experimental.pallas import tpu as pltpu

CHUNK = 64
CONV_DIM = 512
CONV_W = 3
N_HEADS = 8
HEAD_DIM = 64
ATTN_DIM = N_HEADS * HEAD_DIM
N_KV = 2
N_IDX_HEADS = 8
IDX_DIM = 32
TOPK_MAX = 256
Q_BLOCK = 128
N_EXPERTS = 8
EPS = 1e-6
COL_SIZES = (CONV_DIM, CONV_DIM, CONV_DIM, ATTN_DIM, N_KV * HEAD_DIM, N_KV * HEAD_DIM,
             N_IDX_HEADS * IDX_DIM, IDX_DIM, N_IDX_HEADS)
IN_COLS = sum(COL_SIZES)

LANES = 128
V7X_VMEM_LIMIT = 56 * 1024 * 1024

IN_COLS_PAD = ((IN_COLS + LANES - 1) // LANES) * LANES
TAIL_OFF = IN_COLS_PAD - LANES
IW_LANE = IDX_DIM
QSCALE = HEAD_DIM ** -0.5 * math.log2(math.e)
IWSCALE = IDX_DIM ** -0.5 * N_IDX_HEADS ** -0.5
NEG = -1e30
INT_MIN = -2 ** 31
INT_MAX = 2 ** 31 - 1
KEY_CHUNK = 512
BISECT_ROUND = 4
BISECT_FREE_PASSES = 8
EXTRACT_BELOW = 2
BISECT_MAX_ROUNDS = -(-(32 + 3) // BISECT_ROUND) + EXTRACT_BELOW
FIRST_PROBE_GAP = 3 << 23

F32 = jnp.float32
BF16 = jnp.bfloat16
I32 = jnp.int32


def _split_hi_lo(x):
    hi = x.astype(BF16)
    lo = (x - hi.astype(F32)).astype(BF16)
    return hi, lo


def _group_rsqrt(x, g_ref):
    hi, lo = _split_hi_lo(x * x)
    g = g_ref[...]
    ms = jnp.dot(hi, g, preferred_element_type=F32) + jnp.dot(lo, g, preferred_element_type=F32)
    return lax.rsqrt(ms + EPS)


def _rmsnorm(x, g):
    return x * lax.rsqrt(jnp.mean(x * x, axis=-1, keepdims=True) + EPS) * g


def _inproj_kernel(x_ref, prev_ref, g_ref, w_ref, cw_ref, gq_ref, gk_ref, giq_ref, gik_ref,
                   mq_ref, mk_ref, miq_ref, mik_ref, s1_ref, s2_ref, t1_ref, t2_ref,
                   yconv_ref, q_ref, k_ref, v_ref, k2_ref, vb_ref, iqcat_ref, iw_ref, ik_ref,
                   ikcat_ref, cst_ref, carry_ref):
    t = pl.program_id(1)
    nt = pl.num_programs(1)
    x = x_ref[0]
    tm = x.shape[0]
    h = _rmsnorm(x, g_ref[...]).astype(BF16)
    p = jnp.dot(h, w_ref[...], preferred_element_type=F32)
    o = np.cumsum((0,) + COL_SIZES)
    c_gate, b_gate, u = p[:, o[0]:o[1]], p[:, o[1]:o[2]], p[:, o[2]:o[3]]
    q, k, v, iq = p[:, o[3]:o[4]], p[:, o[4]:o[5]], p[:, o[5]:o[6]], p[:, o[6]:o[7]]
    tail = p[:, TAIL_OFF:]

    ci = c_gate * u

    @pl.when(t == 0)
    def _():
        carry_ref[8 - (CONV_W - 1):, :] = prev_ref[0]

    row = lax.broadcasted_iota(I32, ci.shape, 0)
    c7 = carry_ref[7:8, :]
    c6 = carry_ref[6:7, :]
    s1 = jnp.where(row == 0, c7, pltpu.roll(ci, 1, axis=0))
    s2 = jnp.where(row == 0, c6, jnp.where(row == 1, c7, pltpu.roll(ci, 2, axis=0)))
    cw = cw_ref[...]
    conv = cw[0:1] * s2 + cw[1:2] * s1 + cw[2:3] * ci
    yconv_ref[0] = (b_gate * conv).astype(BF16)
    carry_ref[...] = ci[tm - 8:, :]

    @pl.when(t == nt - 1)
    def _():
        cst_ref[0] = ci[tm - (CONV_W - 1):, :]

    qn = q * _group_rsqrt(q, mq_ref) * gq_ref[...]
    q_ref[0] = (qn * QSCALE).astype(BF16)
    kn = k * _group_rsqrt(k, mk_ref) * gk_ref[...]
    k_ref[0] = kn
    v_ref[0] = v
    lane = lax.broadcasted_iota(I32, kn.shape, 1)
    kr = pltpu.roll(kn, HEAD_DIM, axis=1)
    k2_ref[0, :, 0:LANES] = jnp.where(lane < HEAD_DIM, kn, kr).astype(BF16)
    k2_ref[0, :, LANES:2 * LANES] = jnp.where(lane < HEAD_DIM, kr, kn).astype(BF16)
    vb_ref[0] = v.astype(BF16)

    iqn = iq * _group_rsqrt(iq, miq_ref) * giq_ref[...]
    ihi, ilo = _split_hi_lo(iqn)
    iqcat_ref[0] = (jnp.dot(ihi, s1_ref[...], preferred_element_type=F32)
                    + jnp.dot(ilo, s2_ref[...], preferred_element_type=F32)).astype(BF16)
    tn = tail * _group_rsqrt(tail, mik_ref) * gik_ref[...]
    ik_ref[0] = tn[:, 0:IDX_DIM]
    thi, tlo = _split_hi_lo(tn)
    ikcat_ref[0] = (jnp.dot(thi, t1_ref[...], preferred_element_type=F32)
                    + jnp.dot(tlo, t2_ref[...], preferred_element_type=F32)).astype(BF16)
    iw_ref[0] = tail * IWSCALE


def _const_spec(shape):
    nd = len(shape)
    return pl.BlockSpec(shape, lambda *_: (0,) * nd)


def _inproj(x, prev, lw, tm):
    b, t, d = x.shape
    grid = (b, t // tm)
    tile = lambda n: pl.BlockSpec((1, tm, n), lambda i, j: (i, j, 0))
    consts = [lw["attn_g"], lw["w_in"], lw["conv_w"], lw["gq"], lw["gk"], lw["giq"], lw["gik"],
              lw["mq"], lw["mk"], lw["miq"], lw["mik"], lw["s1"], lw["s2"], lw["t1"], lw["t2"]]
    in_specs = ([tile(d), pl.BlockSpec((1, CONV_W - 1, CONV_DIM), lambda i, j: (i, 0, 0))]
                + [_const_spec(c.shape) for c in consts])
    kvd = N_KV * HEAD_DIM
    out_shape = (
        jax.ShapeDtypeStruct((b, t, CONV_DIM), BF16),
        jax.ShapeDtypeStruct((b, t, ATTN_DIM), BF16),
        jax.ShapeDtypeStruct((b, t, kvd), F32),
        jax.ShapeDtypeStruct((b, t, kvd), F32),
        jax.ShapeDtypeStruct((b, t, 2 * kvd), BF16),
        jax.ShapeDtypeStruct((b, t, kvd), BF16),
        jax.ShapeDtypeStruct((b, t, N_IDX_HEADS * LANES), BF16),
        jax.ShapeDtypeStruct((b, t, LANES), F32),
        jax.ShapeDtypeStruct((b, t, IDX_DIM), F32),
        jax.ShapeDtypeStruct((b, t, LANES), BF16),
        jax.ShapeDtypeStruct((b, CONV_W - 1, CONV_DIM), F32),
    )
    out_specs = (tile(CONV_DIM), tile(ATTN_DIM), tile(kvd), tile(kvd), tile(2 * kvd), tile(kvd),
                 tile(N_IDX_HEADS * LANES), tile(LANES), tile(IDX_DIM), tile(LANES),
                 pl.BlockSpec((1, CONV_W - 1, CONV_DIM), lambda i, j: (i, 0, 0)))
    return pl.pallas_call(
        _inproj_kernel, grid=grid, in_specs=in_specs, out_specs=out_specs, out_shape=out_shape,
        scratch_shapes=[pltpu.VMEM((8, CONV_DIM), F32)],
        compiler_params=pltpu.CompilerParams(dimension_semantics=("arbitrary", "arbitrary"),
                                             vmem_limit_bytes=V7X_VMEM_LIMIT),
        name="inproj",
    )(x, prev, *consts)


def _slabs(n):
    return [slice(j * LANES, (j + 1) * LANES) for j in range(n // LANES)]


def _tree(op, xs):
    while len(xs) > 1:
        xs = [op(xs[i], xs[i + 1]) for i in range(0, len(xs) - 1, 2)] + ([xs[-1]] if len(xs) % 2 else [])
    return xs[0]


def _attn_kernel(q_ref, iqcat_ref, iw_ref, k2_ref, v_ref, ikcat_ref, tri_ref, o_ref,
                 keyst_ref, keys_ref, lga_ref, lgb_ref, lhs_ref, wb_ref, qg_ref, sa_ref, sb_ref,
                 mxa_ref, mxb_ref, m_ref, l_ref, acc_ref,
                 *, tq, kc, n_keys, q_pos0, topk):
    qi = pl.program_id(1)
    q0 = q_pos0 + qi * tq
    shift = CHUNK.bit_length() - 1

    def key_limit(pos):
        return jnp.minimum(((pos >> shift) + 1) << shift, n_keys)

    lane = lax.broadcasted_iota(I32, (tq, LANES), 1)
    lim = key_limit(q0 + lax.broadcasted_iota(I32, (tq, LANES), 0))
    qlane = lax.broadcasted_iota(I32, (1, LANES), 1)
    lim_t = jnp.where(qlane < tq, key_limit(q0 + qlane), 0)
    n_chunks = (key_limit(q0 + tq - 1) + kc - 1) // kc
    kfl = jnp.minimum(lim_t, topk).astype(F32)

    iqc = iqcat_ref[0]
    for h in range(N_IDX_HEADS):
        lhs_ref[h * tq:(h + 1) * tq, :] = iqc[:, h * LANES:(h + 1) * LANES]
    iw = iw_ref[0]
    for h in range(N_IDX_HEADS):
        col = jnp.sum(jnp.where(lane == IW_LANE + h, iw, 0.0), axis=1, keepdims=True)
        wb_ref[h] = jnp.broadcast_to(col, (tq, LANES))
    pad_rows = jnp.full((LANES - tq, LANES), INT_MIN, I32) if tq < LANES else None

    last_chunk = keys_ref.shape[0] - 1

    def logits(c, lg_ref):
        off = pl.multiple_of(jnp.minimum(c, last_chunk) * kc, kc)
        lg_ref[...] = lax.dot_general(lhs_ref[...], ikcat_ref[0, pl.ds(off, kc), :],
                                      (((1,), (1,)), ((), ())), preferred_element_type=F32)

    def keys_from(c, lg_ref, kmax):
        for j, sl in enumerate(_slabs(kc)):
            sc = None
            for h in range(N_IDX_HEADS):
                term = jnp.maximum(lg_ref[h * tq:(h + 1) * tq, sl], 0.0) * wb_ref[h]
                sc = term if sc is None else sc + term
            sc = sc + 0.0
            bits = lax.bitcast_convert_type(sc, I32)
            key = bits ^ ((bits >> 31) & INT_MAX)
            key = jnp.where(c * kc + j * LANES + lane < lim, key, INT_MIN)
            keys_ref[c, :, sl] = key
            if pad_rows is not None:
                key = jnp.concatenate([key, pad_rows], axis=0)
            kt = key.T
            keyst_ref[c, sl, :] = kt
            kmax = jnp.maximum(kmax, _tree(jnp.maximum, [kt[8 * i:8 * i + 8, :]
                                                         for i in range(LANES // 8)]))
        return kmax

    logits(0, lga_ref)

    def score_pair(i, kmax):
        logits(2 * i + 1, lgb_ref)
        kmax = keys_from(2 * i, lga_ref, kmax)
        logits(2 * i + 2, lga_ref)
        return keys_from(2 * i + 1, lgb_ref, kmax)

    odd_chunks = n_chunks % 2 == 1
    kmax = lax.fori_loop(0, n_chunks // 2, score_pair, jnp.full((8, LANES), INT_MIN, I32))
    kmax = lax.cond(odd_chunks, lambda k: keys_from(n_chunks - 1, lga_ref, k), lambda k: k, kmax)
    kmax = jnp.max(kmax, axis=0, keepdims=True)

    tree_sum = functools.partial(_tree, jnp.add)

    def count_ge(thr):
        def body(c, acc):
            hit = jnp.where(keyst_ref[c] >= thr, 1.0, 0.0)
            return acc + tree_sum([hit[8 * i:8 * i + 8, :] for i in range(kc // 8)])

        part = lax.fori_loop(0, n_chunks, body, jnp.zeros((8, LANES), F32))
        return jnp.sum(part, axis=0, keepdims=True)

    lo0 = jnp.full((1, LANES), INT_MIN + 1, I32)

    def next_probe(lo, hi):
        mid = (lo & hi) + ((lo ^ hi) >> 1)
        floor = lo == lo0
        return jnp.where(jnp.logical_and(floor, hi > 1), 1,
                         jnp.where(jnp.logical_and(floor, hi == 1), 0, mid))

    def bisect_pass(lo, hi, clo, chi, mid):
        cnt = count_ge(mid)
        ge = cnt >= kfl
        exact = cnt == kfl
        lo_n = jnp.where(ge, mid, lo)
        hi_n = jnp.where(exact, mid + 1, jnp.where(ge, hi, mid))
        return lo_n, hi_n, jnp.where(ge, cnt, clo), jnp.where(ge, chi, cnt)

    def max_below(bound):
        def body(c, acc):
            kch = keyst_ref[c]
            cand = jnp.where(kch < bound, kch, INT_MIN)
            return jnp.maximum(acc, _tree(jnp.maximum, [cand[8 * i:8 * i + 8, :] for i in range(kc // 8)]))

        part = lax.fori_loop(0, n_chunks, body, jnp.full((8, LANES), INT_MIN, I32))
        return jnp.max(part, axis=0, keepdims=True)

    def bisect_round(state):
        for _ in range(BISECT_ROUND):
            state = bisect_pass(*state, next_probe(state[0], state[1]))
        return state

    def extract_step(state):
        lo, hi, clo, chi = state
        top = max_below(hi)
        cnt = count_ge(top)
        open_ = hi != lo + 1
        done = jnp.logical_and(open_, cnt >= kfl)
        down = jnp.logical_and(open_, cnt < kfl)
        return (jnp.where(done, top, lo), jnp.where(done, top + 1, jnp.where(down, top, hi)),
                jnp.where(done, cnt, clo), jnp.where(down, cnt, chi))

    def open_need(lo, hi, chi):
        need = jnp.where(hi != lo + 1, kfl - chi, 0.0)
        return jnp.max(need, axis=1, keepdims=True)[0, 0]

    def bisect_cond(carry):
        it, need = carry[0], carry[1]
        return jnp.logical_and(need > 0.0, it < BISECT_MAX_ROUNDS)

    def bisect_body(carry):
        it, need = carry[0], carry[1]
        state = lax.cond(need <= EXTRACT_BELOW, extract_step, bisect_round, tuple(carry[2:]))
        return (it + 1, open_need(state[0], state[1], state[3])) + tuple(state)

    hi = jnp.maximum(jnp.minimum(kmax, INT_MAX - 1) + 1, lo0 + 1)
    first = jnp.where(kmax > INT_MIN + 2 + FIRST_PROBE_GAP, kmax - FIRST_PROBE_GAP, next_probe(lo0, hi))
    state = bisect_pass(lo0, hi, lim_t.astype(F32), jnp.zeros((1, LANES), F32), first)
    for _ in range(BISECT_FREE_PASSES):
        state = bisect_pass(*state, next_probe(state[0], state[1]))
    _, _, thr, _, clo, chi = lax.while_loop(
        bisect_cond, bisect_body, (jnp.int32(0), open_need(state[0], state[1], state[3])) + tuple(state))

    extra = clo - kfl
    any_tie = jnp.max(extra, axis=1, keepdims=True)[0, 0]

    @pl.when(any_tie > 0.0)
    def _():
        need = jnp.where(extra > 0.0, kfl - chi, jnp.inf)

        def demote(c, seen):
            kch = keyst_ref[c]
            tied = kch == thr
            rank = seen + jnp.dot(tri_ref[...], jnp.where(tied, 1.0, 0.0).astype(BF16),
                                  preferred_element_type=F32)
            kch = jnp.where(jnp.logical_and(tied, rank > need), kch - 1, kch)
            keyst_ref[c] = kch
            for sl in _slabs(kc):
                keys_ref[c, :, sl] = kch[sl, :].T[:tq, :]
            return rank[kc - 1:kc, :]

        lax.fori_loop(0, n_chunks, demote, jnp.zeros((1, LANES), F32))

    thr_rows = jnp.broadcast_to(thr, (LANES, LANES)).T[:tq, :]
    qv = q_ref[0].astype(F32)
    heads_per_group = N_HEADS // N_KV
    rows = heads_per_group * tq
    for g in range(N_KV):
        for r in range(heads_per_group):
            hd = g * heads_per_group + r
            slab = qv[:, (hd // 2) * LANES:(hd // 2 + 1) * LANES]
            keep = (lane < HEAD_DIM) if hd % 2 == 0 else (lane >= HEAD_DIM)
            qg_ref[g, r * tq:(r + 1) * tq, :] = jnp.where(keep, slab, 0.0).astype(BF16)
    m_ref[...] = jnp.full(m_ref.shape, NEG, F32)
    l_ref[...] = jnp.zeros(l_ref.shape, F32)
    acc_ref[...] = jnp.zeros(acc_ref.shape, F32)

    def qk_scores(c, s_ref, mx_ref):
        c = jnp.minimum(c, last_chunk)
        off = pl.multiple_of(c * kc, kc)
        bias = [jnp.where(keys_ref[c, :, sl] >= thr_rows, 0.0, NEG) for sl in _slabs(kc)]
        for g in range(N_KV):
            kch = k2_ref[0, pl.ds(off, kc), g * LANES:(g + 1) * LANES]
            s = lax.dot_general(qg_ref[g], kch, (((1,), (1,)), ((), ())),
                                preferred_element_type=F32)
            for r in range(heads_per_group):
                rs = slice(r * tq, (r + 1) * tq)
                masked = [s[rs, sl] + b for sl, b in zip(_slabs(kc), bias)]
                for sl, piece in zip(_slabs(kc), masked):
                    s_ref[g, rs, sl] = piece
                part = _tree(jnp.maximum, masked)
                mx_ref[g, rs, :] = jnp.broadcast_to(jnp.max(part, axis=1, keepdims=True), (tq, LANES))

    def softmax_pv(c, s_ref, mx_ref):
        off = pl.multiple_of(c * kc, kc)
        vch = v_ref[0, pl.ds(off, kc), :]
        for g in range(N_KV):
            m_prev = m_ref[g]
            m_new = jnp.maximum(m_prev, mx_ref[g])
            alpha = jnp.exp2(m_prev - m_new)
            probs = []
            for r in range(heads_per_group):
                rs = slice(r * tq, (r + 1) * tq)
                ps = [jnp.exp2(s_ref[g, rs, sl] - m_new[rs]) for sl in _slabs(kc)]
                probs.append(jnp.concatenate(ps, axis=1).astype(BF16))
                l_ref[g, rs, :] = alpha[rs] * l_ref[g, rs, :] + tree_sum(ps)
            acc_ref[g] = alpha * acc_ref[g] + jnp.dot(jnp.concatenate(probs, axis=0), vch,
                                                      preferred_element_type=F32)
            m_ref[g] = m_new

    stage_a = (sa_ref, mxa_ref)
    stage_b = (sb_ref, mxb_ref)
    qk_scores(0, *stage_a)

    def attend_pair(i, carry):
        qk_scores(2 * i + 1, *stage_b)
        softmax_pv(2 * i, *stage_a)
        qk_scores(2 * i + 2, *stage_a)
        softmax_pv(2 * i + 1, *stage_b)
        return carry

    lax.fori_loop(0, n_chunks // 2, attend_pair, 0)

    @pl.when(odd_chunks)
    def _():
        softmax_pv(n_chunks - 1, *stage_a)

    for g in range(N_KV):
        out = acc_ref[g] / jnp.sum(l_ref[g], axis=1, keepdims=True)
        for jj in range(heads_per_group // 2):
            a = out[(2 * jj) * tq:(2 * jj + 1) * tq, :]
            bb = out[(2 * jj + 1) * tq:(2 * jj + 2) * tq, :]
            if g % 2 == 0:
                slab = jnp.where(lane < HEAD_DIM, a, pltpu.roll(bb, HEAD_DIM, axis=1))
            else:
                slab = jnp.where(lane < HEAD_DIM, pltpu.roll(a, HEAD_DIM, axis=1), bb)
            so = (g * heads_per_group // 2 + jj) * LANES
            o_ref[0, :, so:so + LANES] = slab.astype(BF16)


def _attention(q, iqcat, iw, k2, vb, ikcat, *, tq, n_keys, q_pos0):
    b, t, _ = q.shape
    lpad = k2.shape[1]
    kc = KEY_CHUNK
    assert lpad % kc == 0 and t % tq == 0 and tq <= LANES and CHUNK & (CHUNK - 1) == 0
    topk = min(TOPK_MAX, n_keys // 4)
    kern = functools.partial(_attn_kernel, tq=tq, kc=kc, n_keys=n_keys, q_pos0=q_pos0, topk=topk)
    qtile = lambda n: pl.BlockSpec((1, tq, n), lambda i, j: (i, j, 0))
    ktile = lambda n: pl.BlockSpec((1, lpad, n), lambda i, j: (i, 0, 0))
    rows = (N_HEADS // N_KV) * tq
    return pl.pallas_call(
        kern, grid=(b, t // tq),
        in_specs=[qtile(ATTN_DIM), qtile(N_IDX_HEADS * LANES), qtile(LANES),
                  ktile(2 * N_KV * HEAD_DIM), ktile(N_KV * HEAD_DIM), ktile(LANES),
                  pl.BlockSpec((kc, kc), lambda i, j: (0, 0))],
        out_specs=qtile(ATTN_DIM),
        out_shape=jax.ShapeDtypeStruct((b, t, ATTN_DIM), BF16),
        scratch_shapes=[
            pltpu.VMEM((lpad // kc, kc, LANES), I32),
            pltpu.VMEM((lpad // kc, tq, kc), I32),
            pltpu.VMEM((N_IDX_HEADS * tq, kc), F32),
            pltpu.VMEM((N_IDX_HEADS * tq, kc), F32),
            pltpu.VMEM((N_IDX_HEADS * tq, LANES), BF16),
            pltpu.VMEM((N_IDX_HEADS, tq, LANES), F32),
            pltpu.VMEM((N_KV, rows, LANES), BF16),
            pltpu.VMEM((N_KV, rows, kc), F32),
            pltpu.VMEM((N_KV, rows, kc), F32),
            pltpu.VMEM((N_KV, rows, LANES), F32),
            pltpu.VMEM((N_KV, rows, LANES), F32),
            pltpu.VMEM((N_KV, rows, LANES), F32),
            pltpu.VMEM((N_KV, rows, LANES), F32),
            pltpu.VMEM((N_KV, rows, LANES), F32),
        ],
        compiler_params=pltpu.CompilerParams(dimension_semantics=("arbitrary", "arbitrary"),
                                             vmem_limit_bytes=V7X_VMEM_LIMIT),
        name="attention",
    )(q, iqcat, iw, k2, vb, ikcat, jnp.asarray(np.tril(np.ones((kc, kc), np.float32)), BF16))


def _mix_in(x_ref, yc_ref, at_ref, woa_ref, wob_ref, g_ref):
    x1 = (x_ref[...] + jnp.dot(yc_ref[...], woa_ref[...], preferred_element_type=F32)
          + jnp.dot(at_ref[...], wob_ref[...], preferred_element_type=F32))
    return x1, _rmsnorm(x1, g_ref[...])


def _swiglu(hb, wg, wu, wd):
    a = jnp.dot(hb, wg, preferred_element_type=F32)
    b = jnp.dot(hb, wu, preferred_element_type=F32)
    act = (a * (1.0 / (1.0 + jnp.exp(-a))) * b).astype(BF16)
    return jnp.dot(act, wd, preferred_element_type=F32)


def _ffn_dense_kernel(x_ref, yc_ref, at_ref, woa_ref, wob_ref, g_ref, wg_ref, wu_ref, wd_ref, o_ref):
    x1, hn = _mix_in(x_ref, yc_ref, at_ref, woa_ref, wob_ref, g_ref)
    o_ref[...] = x1 + _swiglu(hn.astype(BF16), wg_ref[...], wu_ref[...], wd_ref[...])


def _ffn_dense(x, yc, at, lw, tm):
    m, d = x.shape
    row = lambda n: pl.BlockSpec((tm, n), lambda i: (i, 0))
    consts = [lw["wo_a"], lw["wo_b"], lw["ffn_g"], lw["wg"], lw["wu"], lw["wd"]]
    return pl.pallas_call(
        _ffn_dense_kernel, grid=(m // tm,),
        in_specs=[row(d), row(CONV_DIM), row(ATTN_DIM)]
        + [pl.BlockSpec(c.shape, lambda i: (0, 0), pipeline_mode=pl.Buffered(1)) for c in consts],
        out_specs=row(d), out_shape=jax.ShapeDtypeStruct((m, d), F32),
        compiler_params=pltpu.CompilerParams(dimension_semantics=("arbitrary",),
                                             vmem_limit_bytes=V7X_VMEM_LIMIT),
        name="ffn_dense",
    )(x, yc, at, *consts)


def _ffn_moe_kernel(x_ref, yc_ref, at_ref, woa_ref, wob_ref, g_ref, rhi_ref, rlo_ref, tril_ref,
                    eg_ref, eu_ref, ed_ref, o_ref, h_ref, gate_ref, rank_ref, rankt_ref, cnt_ref, acc_ref,
                    *, cap):
    e = pl.program_id(1)
    tm = x_ref.shape[0]
    lane = lax.broadcasted_iota(I32, (tm, LANES), 1)

    @pl.when(e == 0)
    def _():
        x1, hn = _mix_in(x_ref, yc_ref, at_ref, woa_ref, wob_ref, g_ref)
        hi, lo = _split_hi_lo(hn)
        h_ref[...] = hi
        rhi = rhi_ref[...]
        logits = (jnp.dot(hi, rhi, preferred_element_type=F32)
                  + jnp.dot(lo, rhi, preferred_element_type=F32)
                  + jnp.dot(hi, rlo_ref[...], preferred_element_type=F32))
        logits = jnp.where(lane < N_EXPERTS, logits, -jnp.inf)
        ex = jnp.exp(logits - jnp.max(logits, axis=1, keepdims=True))
        probs = ex / jnp.sum(ex, axis=1, keepdims=True)
        p1 = jnp.max(probs, axis=1, keepdims=True)
        i1 = jnp.min(jnp.where(probs == p1, lane, LANES), axis=1, keepdims=True)
        rest = jnp.where(lane == i1, -1.0, probs)
        p2 = jnp.max(rest, axis=1, keepdims=True)
        i2 = jnp.min(jnp.where(rest == p2, lane, LANES), axis=1, keepdims=True)
        den = p1 + p2
        gate_ref[...] = jnp.where(lane == i1, p1 / den, jnp.where(lane == i2, p2 / den, 0.0))
        routed = jnp.where(jnp.logical_or(lane == i1, lane == i2), 1.0, 0.0)
        before = jnp.dot(tril_ref[...], routed.astype(BF16), preferred_element_type=F32)
        rank = jnp.where(routed > 0.0, before, -1.0)
        rank_ref[...] = rank
        for r0 in range(0, tm, LANES):
            nr = min(LANES, tm - r0)
            blk = rank[r0:r0 + nr, :]
            if nr < LANES:
                blk = jnp.concatenate([blk, jnp.full((LANES - nr, LANES), -1.0, F32)], axis=0)
            rt = blk.T
            for ex_id in range(N_EXPERTS):
                rankt_ref[ex_id, :, r0:r0 + nr] = rt[ex_id:ex_id + 1, :nr]
        cnt_ref[...] = jnp.broadcast_to(jnp.sum(routed, axis=0, keepdims=True), cnt_ref.shape)
        acc_ref[...] = x1

    mine = lane == e
    gate_col = jnp.sum(jnp.where(mine, gate_ref[...], 0.0), axis=1, keepdims=True)
    rank_row = rankt_ref[e]
    n_tok = jnp.sum(jnp.where(mine[:8], cnt_ref[...], 0.0), axis=1, keepdims=True)[0, 0]
    n_sub = (n_tok.astype(I32) + cap - 1) // cap
    slot_r = lax.broadcasted_iota(I32, (cap, tm), 0).astype(F32)

    def sub_block(sb, carry):
        base = (sb * cap).astype(F32)
        gather = jnp.where(rank_row - base == slot_r, 1.0, 0.0).astype(BF16)
        hs = jnp.dot(gather, h_ref[...], preferred_element_type=F32).astype(BF16)
        y = _swiglu(hs, eg_ref[0], eu_ref[0], ed_ref[0]).astype(BF16)
        acc_ref[...] += gate_col * lax.dot_general(gather, y, (((0,), (0,)), ((), ())),
                                                   preferred_element_type=F32)
        return carry

    lax.fori_loop(0, n_sub, sub_block, 0)

    @pl.when(e == pl.num_programs(1) - 1)
    def _():
        o_ref[...] = acc_ref[...]


def _ffn_moe(x, yc, at, lw, tm):
    m, d = x.shape
    ne, _, fe = lw["eg"].shape
    cap = min(tm, -(-(9 * tm // 32) // 16) * 16)
    row = lambda n: pl.BlockSpec((tm, n), lambda i, e: (i, 0))
    tril = jnp.asarray(np.tril(np.ones((tm, tm), np.float32), -1), BF16)
    consts = [lw["wo_a"], lw["wo_b"], lw["ffn_g"], lw["r_hi"], lw["r_lo"], tril]
    return pl.pallas_call(
        functools.partial(_ffn_moe_kernel, cap=cap), grid=(m // tm, ne),
        in_specs=[row(d), row(CONV_DIM), row(ATTN_DIM)]
        + [pl.BlockSpec(c.shape, lambda i, e: (0, 0), pipeline_mode=pl.Buffered(1)) for c in consts]
        + [pl.BlockSpec((1, d, fe), lambda i, e: (e, 0, 0)),
           pl.BlockSpec((1, d, fe), lambda i, e: (e, 0, 0)),
           pl.BlockSpec((1, fe, d), lambda i, e: (e, 0, 0))],
        out_specs=row(d), out_shape=jax.ShapeDtypeStruct((m, d), F32),
        scratch_shapes=[pltpu.VMEM((tm, d), BF16),
                        pltpu.VMEM((tm, LANES), F32),
                        pltpu.VMEM((tm, LANES), F32),
                        pltpu.VMEM((ne, 1, tm), F32),
                        pltpu.VMEM((8, LANES), F32),
                        pltpu.VMEM((tm, d), F32)],
        compiler_params=pltpu.CompilerParams(dimension_semantics=("arbitrary", "arbitrary"),
                                             vmem_limit_bytes=V7X_VMEM_LIMIT),
        name="ffn_moe",
    )(x, yc, at, *consts, lw["eg"], lw["eu"], lw["ed"])


def _block_diag_mean(n, group, valid=None):
    m = np.zeros((n, n), np.float32)
    for s in range(0, n if valid is None else valid, group):
        m[s:s + group, s:s + group] = 1.0 / group
    return jnp.asarray(m, BF16)


def _placement_matrices():
    s1 = np.zeros((N_IDX_HEADS * IDX_DIM, N_IDX_HEADS * LANES), np.float32)
    s2 = np.zeros_like(s1)
    t1 = np.zeros((LANES, LANES), np.float32)
    t2 = np.zeros_like(t1)
    j = np.arange(IDX_DIM)
    for h in range(N_IDX_HEADS):
        s1[h * IDX_DIM + j, h * LANES + j] = 1.0
        s2[h * IDX_DIM + j, h * LANES + IDX_DIM + j] = 1.0
        s1[h * IDX_DIM + j, h * LANES + 2 * IDX_DIM + j] = 1.0
    t1[j, j] = 1.0
    t1[j, IDX_DIM + j] = 1.0
    t2[j, 2 * IDX_DIM + j] = 1.0
    return tuple(jnp.asarray(a, BF16) for a in (s1, s2, t1, t2))


def _layer_weights(l, p):
    d = p["w_in"].shape[1]
    s1, s2, t1, t2 = _placement_matrices()
    lw = {
        "attn_g": p["attn_norm_g"][l][None],
        "w_in": jnp.pad(p["w_in"][l], ((0, 0), (0, IN_COLS_PAD - IN_COLS))).astype(BF16),
        "conv_w": p["conv_w"][l],
        "gq": jnp.tile(p["q_norm_g"][l], N_HEADS)[None],
        "gk": jnp.tile(p["k_norm_g"][l], N_KV)[None],
        "giq": jnp.tile(p["iq_norm_g"][l], N_IDX_HEADS)[None],
        "gik": jnp.pad(p["ik_norm_g"][l], (0, LANES - IDX_DIM))[None],
        "mq": _block_diag_mean(ATTN_DIM, HEAD_DIM),
        "mk": _block_diag_mean(N_KV * HEAD_DIM, HEAD_DIM),
        "miq": _block_diag_mean(N_IDX_HEADS * IDX_DIM, IDX_DIM),
        "mik": _block_diag_mean(LANES, IDX_DIM, valid=IDX_DIM),
        "s1": s1, "s2": s2, "t1": t1, "t2": t2,
        "wo_a": p["w_out"][l][:CONV_DIM].astype(BF16),
        "wo_b": p["w_out"][l][CONV_DIM:].astype(BF16),
        "ffn_g": p["ffn_norm_g"][l][None],
    }
    j = l // 2
    if l % 2 == 0:
        lw.update(wg=p["ffn_w_gate"][j].astype(BF16), wu=p["ffn_w_up"][j].astype(BF16),
                  wd=p["ffn_w_down"][j].astype(BF16))
    else:
        r = jnp.pad(p["router_w"][j], ((0, 0), (0, LANES - N_EXPERTS)))
        r_hi = r.astype(BF16)
        lw.update(r_hi=r_hi, r_lo=(r - r_hi.astype(F32)).astype(BF16),
                  eg=p["moe_w_gate"][j].astype(BF16), eu=p["moe_w_up"][j].astype(BF16),
                  ed=p["moe_w_down"][j].astype(BF16))
    assert d == lw["wo_a"].shape[1]
    return lw


def _pick_tile(n, pref):
    t = min(n, pref)
    while n % t:
        t //= 2
    return t


def _mixer(x, prev, lw, *, tq, past_k2=None, past_v=None, past_ik=None, q_pos0=0):
    b, t, d = x.shape
    yc, q, k, v, k2, vb, iqcat, iw, ik, ikcat, cst = _inproj(x, prev, lw, _pick_tile(t, 1024))
    if past_k2 is not None:
        k2 = jnp.concatenate([past_k2, k2], axis=1)
        vb = jnp.concatenate([past_v, vb], axis=1)
        ikcat = jnp.concatenate([past_ik, ikcat], axis=1)
    n_keys = k2.shape[1]
    pad = (-n_keys) % KEY_CHUNK
    if pad:
        k2, vb, ikcat = (jnp.pad(a, ((0, 0), (0, pad), (0, 0))) for a in (k2, vb, ikcat))
    at = _attention(q, iqcat, iw, k2, vb, ikcat, tq=tq, n_keys=n_keys, q_pos0=q_pos0)
    m = b * t
    x2, yc2, at2 = x.reshape(m, d), yc.reshape(m, CONV_DIM), at.reshape(m, ATTN_DIM)
    if "wg" in lw:
        y = _ffn_dense(x2, yc2, at2, lw, _pick_tile(m, 512))
    else:
        y = _ffn_moe(x2, yc2, at2, lw, _pick_tile(m, 1024))
    return (y.reshape(b, t, d), k.reshape(b, t, N_KV, HEAD_DIM), v.reshape(b, t, N_KV, HEAD_DIM),
            ik, cst)


def kernel(x_prompt, x_sample, cache_k, cache_v, cache_ik, state_conv, attn_norm_g, w_in, conv_w,
           q_norm_g, k_norm_g, iq_norm_g, ik_norm_g, w_out, ffn_norm_g, ffn_w_gate, ffn_w_up,
           ffn_w_down, router_w, moe_w_gate, moe_w_up, moe_w_down):
    params = dict(attn_norm_g=attn_norm_g, w_in=w_in, conv_w=conv_w, q_norm_g=q_norm_g,
                  k_norm_g=k_norm_g, iq_norm_g=iq_norm_g, ik_norm_g=ik_norm_g, w_out=w_out,
                  ffn_norm_g=ffn_norm_g, ffn_w_gate=ffn_w_gate, ffn_w_up=ffn_w_up,
                  ffn_w_down=ffn_w_down, router_w=router_w, moe_w_gate=moe_w_gate,
                  moe_w_up=moe_w_up, moe_w_down=moe_w_down)
    depth = w_in.shape[0]
    xp, xs = x_prompt, x_sample
    bp = xp.shape[0]
    bs, ts, _ = xs.shape
    past = cache_k.shape[2]
    kvd = N_KV * HEAD_DIM
    outs = [[] for _ in range(8)]
    for l in range(depth):
        lw = _layer_weights(l, params)
        prev0 = jnp.zeros((bp, CONV_W - 1, CONV_DIM), F32)
        xp, kp, vp, ikp, cp = _mixer(xp, prev0, lw, tq=Q_BLOCK)

        ck = cache_k[l].reshape(bs, past, kvd)
        g0, g1 = ck[..., :HEAD_DIM], ck[..., HEAD_DIM:]
        past_k2 = jnp.concatenate([g0, g0, g1, g1], axis=-1).astype(BF16)
        past_v = cache_v[l].reshape(bs, past, kvd).astype(BF16)
        cik = cache_ik[l]
        cik_hi = cik.astype(BF16)
        cik_lo = (cik - cik_hi.astype(F32)).astype(BF16)
        past_ik = jnp.concatenate([cik_hi, cik_hi, cik_lo, jnp.zeros_like(cik_hi)], axis=-1)
        xs, ks, vs, iks, cs = _mixer(xs, state_conv[l], lw, tq=ts, past_k2=past_k2,
                                     past_v=past_v, past_ik=past_ik, q_pos0=past)
        for lst, val in zip(outs, (kp, vp, ikp, cp, ks, vs, iks, cs)):
            lst.append(val)
    kp, vp, ikp, cp, ks, vs, iks, cs = (jnp.stack(o) for o in outs)
    return (xp, xs, kp, vp, ikp, cp, ks, vs, iks, cs)
```
